```python
import math
import jax, jax.numpy as jnp
from jax import lax
import numpy as np

D_MODEL = 1024
BATCH = 4
SEQ = 8192
DEPTH = 1

SB_HEADS = 8
SB_HEAD_DIM = 64
SB_WIDTH = SB_HEADS * SB_HEAD_DIM
SB_BLOCK = 128
SW_HEADS = 8
SW_KV_HEADS = 2
SW_GROUP = SW_HEADS // SW_KV_HEADS
SW_HEAD_DIM = 64
SW_Q_WIDTH = SW_HEADS * SW_HEAD_DIM
SW_KV_WIDTH = SW_KV_HEADS * SW_HEAD_DIM
WINDOW = 128
NUM_BUCKETS = 32
MAX_DISTANCE = 128
D_FF = 2816
CONV_WIDTH = 3
EPS = 1e-6
NEG_INF = -1e30

IN_WIDTHS = (SB_WIDTH, SB_WIDTH, SB_WIDTH, SW_Q_WIDTH, SW_KV_WIDTH, SW_KV_WIDTH, 2 * D_MODEL)
IN_TOTAL = sum(IN_WIDTHS)
IN_SPLITS = tuple(int(s) for s in np.cumsum(IN_WIDTHS)[:-1])

kernel_name = "hybrid_stickbreak_swa_convffn_block"


def rms_norm(x, g):
    xf = x.astype(jnp.float32)
    y = xf * lax.rsqrt(jnp.mean(xf * xf, axis=-1, keepdims=True) + EPS)
    return (y * g.astype(jnp.float32)).astype(x.dtype)


def t5_causal_bucket(dist):
    max_exact = NUM_BUCKETS // 2
    is_small = dist < max_exact
    d = jnp.maximum(dist, 1).astype(jnp.float32)
    large = max_exact + (jnp.log(d / max_exact) / math.log(MAX_DISTANCE / max_exact)
                         * (NUM_BUCKETS - max_exact)).astype(jnp.int32)
    large = jnp.minimum(large, NUM_BUCKETS - 1)
    return jnp.where(is_small, dist, large)


def stick_breaking_attention(q, k, v):
    B, S, H, dh = q.shape
    nb = S // SB_BLOCK
    scale = 1.0 / math.sqrt(dh)
    to_blocks = lambda a: a.reshape(B, nb, SB_BLOCK, H, dh).transpose(1, 0, 3, 2, 4)
    qb_all, kb_all, vb_all = to_blocks(q), to_blocks(k), to_blocks(v)
    idx = jnp.arange(SB_BLOCK)

    def per_query_block(args):
        qi, qblk = args

        def body(carry, j):
            acc, log_rem = carry
            ki_raw = qi - j
            block_live = ki_raw >= 0
            ki = jnp.maximum(ki_raw, 0)
            kblk = kb_all[ki]
            vblk = vb_all[ki]
            z = jnp.einsum('bhqd,bhkd->bhqk', qblk, kblk).astype(jnp.float32) * scale
            valid = ((ki * SB_BLOCK + idx[None, :]) < (qi * SB_BLOCK + idx[:, None])) & block_live
            log_beta = jax.nn.log_sigmoid(z)
            log_1mb = jnp.where(valid, jax.nn.log_sigmoid(-z), 0.0)
            later = lax.cumsum(log_1mb, axis=3, reverse=True) - log_1mb
            w = jnp.where(valid, jnp.exp(log_beta + later + log_rem[..., None]), 0.0)
            acc = acc + jnp.einsum('bhqk,bhkd->bhqd', w, vblk.astype(jnp.float32))
            log_rem = log_rem + jnp.sum(log_1mb, axis=-1)
            return (acc, log_rem), None

        init = (jnp.zeros((B, H, SB_BLOCK, dh), jnp.float32),
                jnp.zeros((B, H, SB_BLOCK), jnp.float32))
        (acc, _), _ = lax.scan(body, init, jnp.arange(nb, dtype=jnp.int32))
        return acc

    out = lax.map(per_query_block, (jnp.arange(nb, dtype=jnp.int32), qb_all))
    return out.transpose(1, 0, 3, 2, 4).reshape(B, S, H * dh).astype(q.dtype)


def sliding_window_gqa(q, k, v, sinks, rel_bias):
    B, S, H, dh = q.shape
    nb = S // WINDOW
    scale = 1.0 / math.sqrt(dh)
    qb = q.reshape(B, nb, WINDOW, SW_KV_HEADS, SW_GROUP, dh)
    kb = k.reshape(B, nb, WINDOW, SW_KV_HEADS, dh)
    vb = v.reshape(B, nb, WINDOW, SW_KV_HEADS, dh)
    pad = ((0, 0), (1, 0), (0, 0), (0, 0), (0, 0))
    kk = jnp.concatenate([jnp.pad(kb[:, :-1], pad), kb], axis=2)
    vv = jnp.concatenate([jnp.pad(vb[:, :-1], pad), vb], axis=2)

    qpos = jnp.arange(WINDOW)[:, None] + WINDOW
    kpos = jnp.arange(2 * WINDOW)[None, :]
    dist = qpos - kpos
    valid_local = (dist >= 0) & (dist < WINDOW)
    block_ok = (jnp.arange(nb)[:, None] > 0) | (jnp.arange(2 * WINDOW)[None, :] >= WINDOW)
    valid = valid_local[None] & block_ok[:, None, :]

    bucket = t5_causal_bucket(jnp.maximum(dist, 0))
    bias = rel_bias.astype(jnp.float32)[bucket]
    bias = bias.transpose(2, 0, 1).reshape(SW_KV_HEADS, SW_GROUP, WINDOW, 2 * WINDOW)

    logits = jnp.einsum('bnqhgd,bnkhd->bnhgqk', qb, kk).astype(jnp.float32) * scale + bias
    logits = jnp.where(valid[None, :, None, None], logits, NEG_INF)
    sink = sinks.astype(jnp.float32).reshape(SW_KV_HEADS, SW_GROUP)[None, None, :, :, None, None]
    m = jnp.maximum(jnp.max(logits, axis=-1, keepdims=True), sink)
    p = jnp.exp(logits - m)
    probs = p / (jnp.sum(p, axis=-1, keepdims=True) + jnp.exp(sink - m))
    out = jnp.einsum('bnhgqk,bnkhd->bnqhgd', probs.astype(v.dtype), vv)
    return out.reshape(B, S, H * dh)


def causal_depthwise_conv(u, w, b):
    C = u.shape[-1]
    y = lax.conv_general_dilated(u, w[:, None, :].astype(u.dtype), window_strides=(1,),
                                 padding=[(CONV_WIDTH - 1, 0)],
                                 dimension_numbers=('NWC', 'WIO', 'NWC'),
                                 feature_group_count=C)
    return y + b.astype(u.dtype)


def setup_inputs(seed: int = 0) -> dict:
    key = jax.random.key(seed)
    ks = jax.random.split(key, 16)
    f32 = jnp.float32
    nrm = lambda k, shape, s: jax.random.normal(k, shape, f32) * s
    return {
        "x": nrm(ks[0], (BATCH, SEQ, D_MODEL), 1.0),
        "g_mix": 1.0 + nrm(ks[1], (D_MODEL,), 0.01),
        "w_in": nrm(ks[2], (D_MODEL, IN_TOTAL), D_MODEL ** -0.5),
        "w_sb_proj": nrm(ks[3], (SB_WIDTH, D_MODEL), SB_WIDTH ** -0.5),
        "w_sw_proj": nrm(ks[4], (SW_Q_WIDTH, D_MODEL), SW_Q_WIDTH ** -0.5),
        "w_out": nrm(ks[5], (D_MODEL, D_MODEL), D_MODEL ** -0.5),
        "rel_bias": nrm(ks[6], (NUM_BUCKETS, SW_HEADS), 0.2),
        "sinks": nrm(ks[7], (SW_HEADS,), 1.0),
        "g_ffn": 1.0 + nrm(ks[8], (D_MODEL,), 0.01),
        "w_up": nrm(ks[9], (D_MODEL, 2 * D_FF), D_MODEL ** -0.5),
        "conv_w": nrm(ks[10], (CONV_WIDTH, 2 * D_FF), CONV_WIDTH ** -0.5),
        "conv_b": nrm(ks[11], (2 * D_FF,), 0.01),
        "w_down": nrm(ks[12], (D_FF, D_MODEL), D_FF ** -0.5),
        "g_final": 1.0 + nrm(ks[13], (D_MODEL,), 0.01),
    }


def reference(x, g_mix, w_in, w_sb_proj, w_sw_proj, w_out, rel_bias, sinks,
              g_ffn, w_up, conv_w, conv_b, w_down, g_final):
    B, S, _ = x.shape
    for _layer in range(DEPTH):
        h = rms_norm(x, g_mix)
        proj = h @ w_in
        q_sb, k_sb, v_sb, q_sw, k_sw, v_sw, gate_logits = jnp.split(proj, IN_SPLITS, axis=-1)
        y_sb = stick_breaking_attention(
            q_sb.reshape(B, S, SB_HEADS, SB_HEAD_DIM),
            k_sb.reshape(B, S, SB_HEADS, SB_HEAD_DIM),
            v_sb.reshape(B, S, SB_HEADS, SB_HEAD_DIM))
        y_sw = sliding_window_gqa(
            q_sw.reshape(B, S, SW_HEADS, SW_HEAD_DIM),
            k_sw.reshape(B, S, SW_KV_HEADS, SW_HEAD_DIM),
            v_sw.reshape(B, S, SW_KV_HEADS, SW_HEAD_DIM),
            sinks, rel_bias)
        gates = jax.nn.sigmoid(gate_logits.astype(jnp.float32))
        gate_sb, gate_sw = gates[..., :D_MODEL], gates[..., D_MODEL:]
        merged = (gate_sb * (y_sb @ w_sb_proj).astype(jnp.float32)
                  + gate_sw * (y_sw @ w_sw_proj).astype(jnp.float32)).astype(x.dtype)
        x = x + merged @ w_out
        h2 = rms_norm(x, g_ffn)
        u = causal_depthwise_conv(h2 @ w_up, conv_w, conv_b)
        gate_ff, value_ff = u[..., :D_FF], u[..., D_FF:]
        x = x + (jax.nn.silu(gate_ff) * value_ff) @ w_down
    return rms_norm(x, g_final)
```

```python
import functools
import math

import jax
import jax.numpy as jnp
import numpy as np
from jax import lax
from jax.experimental import pallas as pl
from jax.experimental.pallas import tpu as pltpu

SB_HEADS = 8
SW_HEADS = 8
SW_KV_HEADS = 2
HEAD_DIM = 64
BLOCK = 128
NUM_BUCKETS = 32
MAX_DISTANCE = 128
CONV_WIDTH = 3
EPS = 1e-6
NEG_INF = -1e30
LANES = 128
FF_CHUNK = 256
SB_DEAD_LOG = 104.0
VMEM_LIMIT = 56 * 1024 * 1024

_BF16 = jnp.bfloat16
_F32 = jnp.float32


def _rms(x, g):
    return x * lax.rsqrt(jnp.mean(x * x, axis=-1, keepdims=True) + EPS) * g


def _dot(a, b):
    return jnp.dot(a, b, preferred_element_type=_F32)


def _dot_nt(a, b):
    return lax.dot_general(a, b, (((1,), (1,)), ((), ())), preferred_element_type=_F32)


def _in_proj_kernel(x_ref, g_ref, w_ref, o_ref, *, n_scaled, chunk):
    h = _rms(x_ref[...], g_ref[...]).astype(_BF16)
    scale = 1.0 / math.sqrt(HEAD_DIM)
    for c in range(w_ref.shape[1] // chunk):
        cols = slice(c * chunk, (c + 1) * chunk)
        p = _dot(h, w_ref[:, cols])
        if c in n_scaled:
            p = p * scale
        o_ref[:, cols] = p.astype(_BF16)


def _in_proj(x2d, g, w, *, tm, n_scaled, chunk):
    n, d = x2d.shape
    width = w.shape[1]
    return pl.pallas_call(
        functools.partial(_in_proj_kernel, n_scaled=n_scaled, chunk=chunk),
        grid=(n // tm,),
        in_specs=[
            pl.BlockSpec((tm, d), lambda i: (i, 0)),
            pl.BlockSpec((1, d), lambda i: (0, 0)),
            pl.BlockSpec((d, width), lambda i: (0, 0)),
        ],
        out_specs=pl.BlockSpec((tm, width), lambda i: (i, 0)),
        out_shape=jax.ShapeDtypeStruct((n, width), _BF16),
        compiler_params=pltpu.CompilerParams(
            dimension_semantics=("arbitrary",), vmem_limit_bytes=VMEM_LIMIT),
        name="in_proj",
    )(x2d, g, w)


def _sb_kernel(q_ref, k_ref, v_ref, t_ref, o_ref):
    qi = pl.program_id(2)
    q = q_ref[0]
    tri = t_ref[...]
    lane = lax.broadcasted_iota(jnp.int32, (BLOCK, LANES), 1)
    row = lax.broadcasted_iota(jnp.int32, (BLOCK, BLOCK), 0)
    col = lax.broadcasted_iota(jnp.int32, (BLOCK, BLOCK), 1)
    strictly_earlier = col < row

    def visit(qh, kb, rem, acc, diagonal):
        start = pl.multiple_of(kb * BLOCK, BLOCK)
        k = k_ref[0, pl.ds(start, BLOCK), :]
        v = v_ref[0, pl.ds(start, BLOCK), :]
        z = _dot_nt(qh, k)
        sp = jnp.maximum(z, 0.0) + jnp.log(1.0 + jnp.exp(-jnp.abs(z)))
        if diagonal:
            sp = jnp.where(strictly_earlier, sp, 0.0)
        hi = sp.astype(_BF16)
        lo = (sp - hi.astype(_F32)).astype(_BF16)
        cum = _dot(jnp.concatenate([hi, lo], axis=1), tri)
        w = jnp.exp(z - cum[:, :BLOCK] - rem)
        if diagonal:
            w = jnp.where(strictly_earlier, w, 0.0)
        acc = acc + _dot(w.astype(_BF16), v)
        return rem + cum[:, BLOCK:], acc

    outs = []
    for half in range(LANES // HEAD_DIM):
        in_head = (lane >= half * HEAD_DIM) & (lane < (half + 1) * HEAD_DIM)
        qh = jnp.where(in_head, q, jnp.zeros_like(q))
        rem0 = jnp.zeros((BLOCK, BLOCK), _F32)
        acc0 = jnp.zeros((BLOCK, LANES), _F32)
        rem, acc = visit(qh, qi, rem0, acc0, True)

        def cond(carry):
            kb, rem, _ = carry
            return (kb >= 0) & (jnp.min(rem) < SB_DEAD_LOG)

        def body(carry, qh=qh):
            kb, rem, acc = carry
            rem, acc = visit(qh, kb, rem, acc, False)
            return kb - 1, rem, acc

        _, _, acc = lax.while_loop(cond, body, (qi - 1, rem, acc))
        outs.append(acc)
    o_ref[0] = jnp.where(lane < HEAD_DIM, outs[0], outs[1]).astype(o_ref.dtype)


def _suffix_sum_matrix():
    j = np.arange(2 * BLOCK)[:, None] % BLOCK
    s = np.arange(2 * BLOCK)[None, :]
    return jnp.asarray(((s >= BLOCK) | (j >= s)).astype(np.float32), dtype=_BF16)


def _sb_attn(qkv, *, q_col, k_col, v_col):
    b, s, _ = qkv.shape
    pairs = SB_HEADS * HEAD_DIM // LANES
    return pl.pallas_call(
        _sb_kernel,
        grid=(b, pairs, s // BLOCK),
        in_specs=[
            pl.BlockSpec((1, BLOCK, LANES), lambda bi, p, qi: (bi, qi, q_col + p)),
            pl.BlockSpec((1, s, LANES), lambda bi, p, qi: (bi, 0, k_col + p)),
            pl.BlockSpec((1, s, LANES), lambda bi, p, qi: (bi, 0, v_col + p)),
            pl.BlockSpec((2 * BLOCK, 2 * BLOCK), lambda bi, p, qi: (0, 0)),
        ],
        out_specs=pl.BlockSpec((1, BLOCK, LANES), lambda bi, p, qi: (bi, qi, p)),
        out_shape=jax.ShapeDtypeStruct((b, s, SB_HEADS * HEAD_DIM), _BF16),
        compiler_params=pltpu.CompilerParams(
            dimension_semantics=("arbitrary", "arbitrary", "arbitrary"),
            vmem_limit_bytes=VMEM_LIMIT),
        name="sb_attn",
    )(qkv, qkv, qkv, _suffix_sum_matrix())


def _sw_bias_kernel(bucket_ref, rel_ref, o_ref):
    bucket = bucket_ref[...]
    for h in range(SW_HEADS):
        acc = jnp.zeros(bucket.shape, _F32)
        for b in range(NUM_BUCKETS):
            acc = jnp.where(bucket == b, rel_ref[b, h], acc)
        o_ref[h] = acc


def _t5_bucket(dist):
    max_exact = NUM_BUCKETS // 2
    d = jnp.maximum(dist, 1).astype(_F32)
    large = max_exact + (jnp.log(d / max_exact) / math.log(MAX_DISTANCE / max_exact)
                         * (NUM_BUCKETS - max_exact)).astype(jnp.int32)
    large = jnp.minimum(large, NUM_BUCKETS - 1)
    return jnp.where(dist < max_exact, dist, large)


def _sw_bias(rel_bias):
    dist = (jnp.arange(BLOCK)[:, None] + BLOCK) - jnp.arange(2 * BLOCK)[None, :]
    bucket = _t5_bucket(jnp.maximum(dist, 0)).astype(jnp.int32)
    return pl.pallas_call(
        _sw_bias_kernel,
        in_specs=[
            pl.BlockSpec(memory_space=pltpu.VMEM),
            pl.BlockSpec(memory_space=pltpu.SMEM),
        ],
        out_specs=pl.BlockSpec(memory_space=pltpu.VMEM),
        out_shape=jax.ShapeDtypeStruct((SW_HEADS, BLOCK, 2 * BLOCK), _F32),
        name="sw_bias",
    )(bucket, rel_bias.astype(_F32))


def _sw_kernel(sink_ref, q_ref, kp_ref, kc_ref, vp_ref, vc_ref, bias_ref, o_ref):
    n = pl.program_id(1)
    lane = lax.broadcasted_iota(jnp.int32, (BLOCK, LANES), 1)
    qpos = lax.broadcasted_iota(jnp.int32, (BLOCK, 2 * BLOCK), 0) + BLOCK
    kpos = lax.broadcasted_iota(jnp.int32, (BLOCK, 2 * BLOCK), 1)
    dist = qpos - kpos
    first_key = jnp.where(n > 0, 0, BLOCK)
    valid = (dist >= 0) & (dist < BLOCK) & (kpos >= first_key)
    group = SW_HEADS // SW_KV_HEADS
    heads_per_tile = LANES // HEAD_DIM
    for c in range(SW_HEADS // heads_per_tile):
        g = (c * heads_per_tile) // group
        cols = slice(g * LANES, (g + 1) * LANES)
        kk = jnp.concatenate([kp_ref[0, :, cols], kc_ref[0, :, cols]], axis=0)
        vv = jnp.concatenate([vp_ref[0, :, cols], vc_ref[0, :, cols]], axis=0)
        q = q_ref[0, :, c * LANES:(c + 1) * LANES]
        outs = []
        for half in range(heads_per_tile):
            h = c * heads_per_tile + half
            in_head = (lane >= half * HEAD_DIM) & (lane < (half + 1) * HEAD_DIM)
            qh = jnp.where(in_head, q, jnp.zeros_like(q))
            logits = _dot_nt(qh, kk) + bias_ref[h]
            logits = jnp.where(valid, logits, NEG_INF)
            sink = sink_ref[h]
            m = jnp.maximum(jnp.max(logits, axis=-1, keepdims=True), sink)
            p = jnp.exp(logits - m)
            denom = jnp.sum(p, axis=-1, keepdims=True) + jnp.exp(sink - m)
            probs = p * (1.0 / denom)
            outs.append(_dot(probs.astype(_BF16), vv))
        o_ref[0, :, c * LANES:(c + 1) * LANES] = jnp.where(
            lane < HEAD_DIM, outs[0], outs[1]).astype(o_ref.dtype)


def _sw_attn(qkv, sinks, bias, *, q_col, k_col, v_col):
    b, s, _ = qkv.shape
    qw = SW_HEADS * HEAD_DIM
    kw = SW_KV_HEADS * LANES
    prev = lambda bi, n: (bi, jnp.maximum(n - 1, 0))
    return pl.pallas_call(
        _sw_kernel,
        grid=(b, s // BLOCK),
        in_specs=[
            pl.BlockSpec(memory_space=pltpu.SMEM),
            pl.BlockSpec((1, BLOCK, qw), lambda bi, n: (bi, n, q_col)),
            pl.BlockSpec((1, BLOCK, kw), lambda bi, n: prev(bi, n) + (k_col,)),
            pl.BlockSpec((1, BLOCK, kw), lambda bi, n: (bi, n, k_col)),
            pl.BlockSpec((1, BLOCK, kw), lambda bi, n: prev(bi, n) + (v_col,)),
            pl.BlockSpec((1, BLOCK, kw), lambda bi, n: (bi, n, v_col)),
            pl.BlockSpec((SW_HEADS, BLOCK, 2 * BLOCK), lambda bi, n: (0, 0, 0)),
        ],
        out_specs=pl.BlockSpec((1, BLOCK, qw), lambda bi, n: (bi, n, 0)),
        out_shape=jax.ShapeDtypeStruct((b, s, qw), _BF16),
        compiler_params=pltpu.CompilerParams(
            dimension_semantics=("arbitrary", "arbitrary"), vmem_limit_bytes=VMEM_LIMIT),
        name="sw_attn",
    )(sinks.astype(_F32), qkv, qkv, qkv, qkv, qkv, bias)


def _mix_kernel(x_ref, g_ref, wg_ref, ysb_ref, ysw_ref, wsb_ref, wsw_ref, wo_ref, o_ref):
    x = x_ref[...]
    d = x.shape[1]
    h = _rms(x, g_ref[...]).astype(_BF16)
    gates = jax.nn.sigmoid(_dot(h, wg_ref[...]))
    merged = (gates[:, :d] * _dot(ysb_ref[...], wsb_ref[...])
              + gates[:, d:] * _dot(ysw_ref[...], wsw_ref[...]))
    o_ref[...] = x + _dot(merged.astype(_BF16), wo_ref[...])


def _mix_out(x2d, g, w_gate, y_sb, y_sw, w_sb, w_sw, w_out, *, tm):
    n, d = x2d.shape
    full = lambda a: pl.BlockSpec(a.shape, lambda i: (0, 0))
    rows = lambda a: pl.BlockSpec((tm, a.shape[1]), lambda i: (i, 0))
    return pl.pallas_call(
        _mix_kernel,
        grid=(n // tm,),
        in_specs=[rows(x2d), full(g), full(w_gate), rows(y_sb), rows(y_sw),
                  full(w_sb), full(w_sw), full(w_out)],
        out_specs=pl.BlockSpec((tm, d), lambda i: (i, 0)),
        out_shape=jax.ShapeDtypeStruct((n, d), _F32),
        compiler_params=pltpu.CompilerParams(
            dimension_semantics=("arbitrary",), vmem_limit_bytes=VMEM_LIMIT),
        name="mix_out",
    )(x2d, g, w_gate, y_sb, y_sw, w_sb, w_sw, w_out)


HALO = 8


def _ffn_kernel(x_ref, g_ref, wu_ref, cw_ref, cb_ref, wd_ref, gf_ref, o_ref,
                h_scr, acc_scr, u_scr, carry_scr, *, tiles_per_seq):
    i = pl.program_id(0)
    j = pl.program_id(1)
    tm = x_ref.shape[0]

    @pl.when(j == 0)
    def _():
        h_scr[...] = _rms(x_ref[...], g_ref[...]).astype(_BF16)
        acc_scr[...] = jnp.zeros_like(acc_scr)

    @pl.when(i % tiles_per_seq == 0)
    def _():
        carry_scr[j] = jnp.zeros(carry_scr.shape[1:], _F32)

    u = _dot(h_scr[...], wu_ref[...])
    u_scr[HALO:HALO + tm, :] = u
    u_scr[0:HALO, :] = carry_scr[j]
    carry_scr[j] = u[tm - HALO:tm, :]
    y = cb_ref[...] + cw_ref[CONV_WIDTH - 1:CONV_WIDTH, :] * u
    for tap in range(CONV_WIDTH - 1):
        back = CONV_WIDTH - 1 - tap
        y = y + cw_ref[tap:tap + 1, :] * u_scr[HALO - back:HALO - back + tm, :]
    act = jax.nn.silu(y[:, :FF_CHUNK]) * y[:, FF_CHUNK:]
    acc_scr[...] += _dot(act.astype(_BF16), wd_ref[...])

    @pl.when(j == pl.num_programs(1) - 1)
    def _():
        o_ref[...] = _rms(x_ref[...] + acc_scr[...], gf_ref[...])


def _conv_ffn(x1, g_ffn, w_up, conv_w, conv_b, w_down, g_final, *, tm, tiles_per_seq):
    n, d = x1.shape
    d_ff = w_down.shape[0]
    nj = d_ff // FF_CHUNK
    return pl.pallas_call(
        functools.partial(_ffn_kernel, tiles_per_seq=tiles_per_seq),
        grid=(n // tm, nj),
        in_specs=[
            pl.BlockSpec((tm, d), lambda i, j: (i, 0)),
            pl.BlockSpec((1, d), lambda i, j: (0, 0)),
            pl.BlockSpec((d, 2 * FF_CHUNK), lambda i, j: (0, j)),
            pl.BlockSpec((CONV_WIDTH, 2 * FF_CHUNK), lambda i, j: (0, j)),
            pl.BlockSpec((1, 2 * FF_CHUNK), lambda i, j: (0, j)),
            pl.BlockSpec((FF_CHUNK, d), lambda i, j: (j, 0)),
            pl.BlockSpec((1, d), lambda i, j: (0, 0)),
        ],
        out_specs=pl.BlockSpec((tm, d), lambda i, j: (i, 0)),
        out_shape=jax.ShapeDtypeStruct((n, d), _F32),
        scratch_shapes=[
            pltpu.VMEM((tm, d), _BF16),
            pltpu.VMEM((tm, d), _F32),
            pltpu.VMEM((HALO + tm, 2 * FF_CHUNK), _F32),
            pltpu.VMEM((nj, HALO, 2 * FF_CHUNK), _F32),
        ],
        compiler_params=pltpu.CompilerParams(
            dimension_semantics=("arbitrary", "arbitrary"), vmem_limit_bytes=VMEM_LIMIT),
        name="conv_ffn",
    )(x1, g_ffn, w_up, conv_w, conv_b, w_down, g_final)


def _interleave_ff(a, d_ff):
    lead = a.shape[:-1]
    a = a.reshape(lead + (2, d_ff // FF_CHUNK, FF_CHUNK))
    return jnp.swapaxes(a, -3, -2).reshape(lead + (2 * d_ff,))


def kernel(x, g_mix, w_in, w_sb_proj, w_sw_proj, w_out, rel_bias, sinks, g_ffn, w_up,
           conv_w, conv_b, w_down, g_final):
    b, s, d = x.shape
    n = b * s
    d_ff = w_down.shape[0]
    sb_w = SB_HEADS * HEAD_DIM
    sw_qw = SW_HEADS * HEAD_DIM
    sw_kw = SW_KV_HEADS * HEAD_DIM
    assert s % BLOCK == 0 and d_ff % FF_CHUNK == 0
    tm = min(512, s)
    assert s % tm == 0

    qkv_end = 3 * sb_w + sw_qw
    k_sw = w_in[:, qkv_end:qkv_end + sw_kw].reshape(d, SW_KV_HEADS, 1, HEAD_DIM)
    v_sw = w_in[:, qkv_end + sw_kw:qkv_end + 2 * sw_kw].reshape(d, SW_KV_HEADS, 1, HEAD_DIM)
    dup = lambda a: jnp.broadcast_to(
        a, (d, SW_KV_HEADS, LANES // HEAD_DIM, HEAD_DIM)).reshape(d, SW_KV_HEADS * LANES)
    w_qkv = jnp.concatenate([w_in[:, :qkv_end], dup(k_sw), dup(v_sw)], axis=1).astype(_BF16)
    w_gate = w_in[:, qkv_end + 2 * sw_kw:].astype(_BF16)
    row = lambda a: a.reshape(1, -1).astype(_F32)

    x2d = x.reshape(n, d)
    chunk = 512
    qkv = _in_proj(x2d, row(g_mix), w_qkv, tm=tm, chunk=chunk,
                   n_scaled=(0, 3 * sb_w // chunk))
    qkv = qkv.reshape(b, s, -1)
    y_sb = _sb_attn(qkv, q_col=0, k_col=sb_w // LANES, v_col=2 * sb_w // LANES)
    bias = _sw_bias(rel_bias)
    kv_tile = SW_KV_HEADS * LANES
    y_sw = _sw_attn(qkv, sinks, bias, q_col=3 * sb_w // sw_qw,
                    k_col=qkv_end // kv_tile, v_col=qkv_end // kv_tile + 1)
    x1 = _mix_out(x2d, row(g_mix), w_gate, y_sb.reshape(n, sb_w), y_sw.reshape(n, sw_qw),
                  w_sb_proj.astype(_BF16), w_sw_proj.astype(_BF16), w_out.astype(_BF16), tm=tm)
    out = _conv_ffn(x1, row(g_ffn), _interleave_ff(w_up, d_ff).astype(_BF16),
                    _interleave_ff(conv_w, d_ff).astype(_F32),
                    _interleave_ff(conv_b, d_ff).reshape(1, -1).astype(_F32),
                    w_down.astype(_BF16), row(g_final), tm=tm, tiles_per_seq=s // tm)
    return out.reshape(b, s, d)
```

```python
import functools
import math

import jax
import jax.numpy as jnp
import numpy as np
from jax import lax
from jax.experimental import pallas as pl
from jax.experimental.pallas import tpu as pltpu

SB_HEADS = 8
SW_HEADS = 8
SW_KV_HEADS = 2
HEAD_DIM = 64
BLOCK = 128
NUM_BUCKETS = 32
MAX_DISTANCE = 128
CONV_WIDTH = 3
EPS = 1e-6
NEG_INF = -1e30
LANES = 128
FF_CHUNK = 256
VMEM_LIMIT = 56 * 1024 * 1024

_BF16 = jnp.bfloat16
_F32 = jnp.float32


def _rms(x, g):
    return x * lax.rsqrt(jnp.mean(x * x, axis=-1, keepdims=True) + EPS) * g


def _dot(a, b):
    return jnp.dot(a, b, preferred_element_type=_F32)


def _dot_nt(a, b):
    return lax.dot_general(a, b, (((1,), (1,)), ((), ())), preferred_element_type=_F32)


def _in_proj_kernel(x_ref, g_ref, w_ref, o_ref, *, n_scaled, chunk):
    h = _rms(x_ref[...], g_ref[...]).astype(_BF16)
    scale = 1.0 / math.sqrt(HEAD_DIM)
    for c in range(w_ref.shape[1] // chunk):
        cols = slice(c * chunk, (c + 1) * chunk)
        p = _dot(h, w_ref[:, cols])
        if c in n_scaled:
            p = p * scale
        o_ref[:, cols] = p.astype(_BF16)


def _in_proj(x2d, g, w, *, tm, n_scaled, chunk):
    n, d = x2d.shape
    width = w.shape[1]
    return pl.pallas_call(
        functools.partial(_in_proj_kernel, n_scaled=n_scaled, chunk=chunk),
        grid=(n // tm,),
        in_specs=[
            pl.BlockSpec((tm, d), lambda i: (i, 0)),
            pl.BlockSpec((1, d), lambda i: (0, 0)),
            pl.BlockSpec((d, width), lambda i: (0, 0)),
        ],
        out_specs=pl.BlockSpec((tm, width), lambda i: (i, 0)),
        out_shape=jax.ShapeDtypeStruct((n, width), _BF16),
        compiler_params=pltpu.CompilerParams(
            dimension_semantics=("arbitrary",), vmem_limit_bytes=VMEM_LIMIT),
        name="in_proj",
    )(x2d, g, w)


SB_SUB = 8
SB_FIXED = 3
SB_OFF = 1e30
LOG2E = 1.4426950408889634
SB_DEAD_LOG2 = 151.0


def _sb_kernel(q_ref, k_ref, v_ref, t_ref, o_ref, rem_scr, acc_scr):
    base = pl.program_id(2) * SB_SUB
    tri = t_ref[...]
    rows2 = 2 * BLOCK
    lane = lax.broadcasted_iota(jnp.int32, (rows2, LANES), 1)
    row2 = lax.broadcasted_iota(jnp.int32, (rows2, LANES), 0)
    own_lanes = (lane >= HEAD_DIM) == (row2 >= BLOCK)
    strictly_earlier = lane < (row2 & (BLOCK - 1))

    def stacked_q(a):
        q = q_ref[0, a * BLOCK:(a + 1) * BLOCK, :]
        q2 = jnp.concatenate([q, q], axis=0)
        return jnp.where(own_lanes, q2, jnp.zeros_like(q2))

    def key_rows(kb):
        return pl.ds(pl.multiple_of(kb * BLOCK, BLOCK), BLOCK)

    def scores(q2, kb):
        return _dot_nt(q2, k_ref[0, key_rows(kb), :]) * LOG2E

    def suffix_mass(z, diagonal):
        sp = jnp.maximum(z, 0.0) + jnp.log2(1.0 + jnp.exp2(-jnp.abs(z)))
        if diagonal:
            sp = jnp.where(strictly_earlier, sp, 0.0)
        hi = sp.astype(_BF16)
        lo = (sp - hi.astype(_F32)).astype(_BF16)
        return _dot(jnp.concatenate([hi, lo], axis=1), tri)

    def weighted_values(z, cum, rem, kb, diagonal):
        w = jnp.exp2(z - cum[:, :BLOCK] - rem)
        if diagonal:
            w = jnp.where(strictly_earlier, w, 0.0)
        return _dot(w.astype(_BF16), v_ref[0, key_rows(kb), :])

    def visit(q2, kb, rem, acc, diagonal):
        z = scores(q2, kb)
        cum = suffix_mass(z, diagonal)
        return rem + cum[:, BLOCK:], acc + weighted_values(z, cum, rem, kb, diagonal)

    def earlier_block(a, t, rem):
        kb = base + a - t
        return jnp.maximum(kb, 0), rem + jnp.where(kb < 0, SB_OFF, 0.0)

    items = [(a, t) for a in range(SB_SUB) for t in range(SB_FIXED)]
    q2s, zs, cums = {}, {}, {}
    rems = [jnp.zeros((rows2, BLOCK), _F32) for _ in range(SB_SUB)]
    accs = [jnp.zeros((rows2, LANES), _F32) for _ in range(SB_SUB)]
    for step in range(len(items) + 2):
        if step < len(items):
            a, t = items[step]
            if t == 0:
                q2s[a] = stacked_q(a)
            zs[step] = scores(q2s[a], jnp.maximum(base + a - t, 0))
        if 0 <= step - 1 < len(items):
            cums[step - 1] = suffix_mass(zs[step - 1], items[step - 1][1] == 0)
        if 0 <= step - 2 < len(items):
            a, t = items[step - 2]
            kb, rem = earlier_block(a, t, rems[a])
            accs[a] = accs[a] + weighted_values(zs.pop(step - 2), cums[step - 2], rem, kb, t == 0)
            rems[a] = rem + cums.pop(step - 2)[:, BLOCK:]
    floor = None
    for a in range(SB_SUB):
        rem_scr[a] = rems[a]
        acc_scr[a] = accs[a]
        floor = rems[a] if floor is None else jnp.minimum(floor, rems[a])

    def cond(carry):
        t, least = carry
        return (base + SB_SUB - 1 - t >= 0) & (least < SB_DEAD_LOG2)

    def body(carry):
        t, _ = carry
        floor = None
        for a in range(SB_SUB):
            kb, rem = earlier_block(a, t, rem_scr[a])
            rem, acc = visit(stacked_q(a), kb, rem, acc_scr[a], False)
            rem_scr[a] = rem
            acc_scr[a] = acc
            floor = rem if floor is None else jnp.minimum(floor, rem)
        return t + 1, jnp.min(floor)

    lax.while_loop(cond, body, (jnp.int32(SB_FIXED), jnp.min(floor)))
    head0_lanes = lax.broadcasted_iota(jnp.int32, (BLOCK, LANES), 1) < HEAD_DIM
    for a in range(SB_SUB):
        o_ref[0, a * BLOCK:(a + 1) * BLOCK, :] = jnp.where(
            head0_lanes, acc_scr[a, :BLOCK, :], acc_scr[a, BLOCK:, :]).astype(o_ref.dtype)


def _suffix_sum_matrix():
    j = np.arange(2 * BLOCK)[:, None] % BLOCK
    s = np.arange(2 * BLOCK)[None, :]
    return jnp.asarray(((s >= BLOCK) | (j >= s)).astype(np.float32), dtype=_BF16)


def _sb_attn(qkv, *, q_col, k_col, v_col):
    b, s, _ = qkv.shape
    pairs = SB_HEADS * HEAD_DIM // LANES
    tq = SB_SUB * BLOCK
    assert s % tq == 0
    return pl.pallas_call(
        _sb_kernel,
        grid=(b, pairs, s // tq),
        in_specs=[
            pl.BlockSpec((1, tq, LANES), lambda bi, p, qi: (bi, qi, q_col + p)),
            pl.BlockSpec((1, s, LANES), lambda bi, p, qi: (bi, 0, k_col + p)),
            pl.BlockSpec((1, s, LANES), lambda bi, p, qi: (bi, 0, v_col + p)),
            pl.BlockSpec((2 * BLOCK, 2 * BLOCK), lambda bi, p, qi: (0, 0)),
        ],
        out_specs=pl.BlockSpec((1, tq, LANES), lambda bi, p, qi: (bi, qi, p)),
        out_shape=jax.ShapeDtypeStruct((b, s, SB_HEADS * HEAD_DIM), _BF16),
        scratch_shapes=[
            pltpu.VMEM((SB_SUB, 2 * BLOCK, BLOCK), _F32),
            pltpu.VMEM((SB_SUB, 2 * BLOCK, LANES), _F32),
        ],
        compiler_params=pltpu.CompilerParams(
            dimension_semantics=("arbitrary", "arbitrary", "arbitrary"),
            vmem_limit_bytes=VMEM_LIMIT),
        name="sb_attn",
    )(qkv, qkv, qkv, _suffix_sum_matrix())


def _sw_bias_kernel(bucket_ref, rel_ref, o_ref):
    bucket = bucket_ref[...]
    for h in range(SW_HEADS):
        acc = jnp.zeros(bucket.shape, _F32)
        for b in range(NUM_BUCKETS):
            acc = jnp.where(bucket == b, rel_ref[b, h], acc)
        o_ref[h] = acc


def _t5_bucket(dist):
    max_exact = NUM_BUCKETS // 2
    d = jnp.maximum(dist, 1).astype(_F32)
    large = max_exact + (jnp.log(d / max_exact) / math.log(MAX_DISTANCE / max_exact)
                         * (NUM_BUCKETS - max_exact)).astype(jnp.int32)
    large = jnp.minimum(large, NUM_BUCKETS - 1)
    return jnp.where(dist < max_exact, dist, large)


def _sw_bias(rel_bias):
    dist = (jnp.arange(BLOCK)[:, None] + BLOCK) - jnp.arange(2 * BLOCK)[None, :]
    bucket = _t5_bucket(jnp.maximum(dist, 0)).astype(jnp.int32)
    return pl.pallas_call(
        _sw_bias_kernel,
        in_specs=[
            pl.BlockSpec(memory_space=pltpu.VMEM),
            pl.BlockSpec(memory_space=pltpu.SMEM),
        ],
        out_specs=pl.BlockSpec(memory_space=pltpu.VMEM),
        out_shape=jax.ShapeDtypeStruct((SW_HEADS, BLOCK, 2 * BLOCK), _F32),
        name="sw_bias",
    )(bucket, rel_bias.astype(_F32))


def _sw_kernel(sink_ref, q_ref, kp_ref, kc_ref, vp_ref, vc_ref, bias_ref, o_ref):
    n = pl.program_id(1)
    lane = lax.broadcasted_iota(jnp.int32, (BLOCK, LANES), 1)
    qpos = lax.broadcasted_iota(jnp.int32, (BLOCK, 2 * BLOCK), 0) + BLOCK
    kpos = lax.broadcasted_iota(jnp.int32, (BLOCK, 2 * BLOCK), 1)
    dist = qpos - kpos
    first_key = jnp.where(n > 0, 0, BLOCK)
    valid = (dist >= 0) & (dist < BLOCK) & (kpos >= first_key)
    group = SW_HEADS // SW_KV_HEADS
    heads_per_tile = LANES // HEAD_DIM
    for c in range(SW_HEADS // heads_per_tile):
        g = (c * heads_per_tile) // group
        cols = slice(g * LANES, (g + 1) * LANES)
        kk = jnp.concatenate([kp_ref[0, :, cols], kc_ref[0, :, cols]], axis=0)
        vv = jnp.concatenate([vp_ref[0, :, cols], vc_ref[0, :, cols]], axis=0)
        q = q_ref[0, :, c * LANES:(c + 1) * LANES]
        outs = []
        for half in range(heads_per_tile):
            h = c * heads_per_tile + half
            in_head = (lane >= half * HEAD_DIM) & (lane < (half + 1) * HEAD_DIM)
            qh = jnp.where(in_head, q, jnp.zeros_like(q))
            logits = _dot_nt(qh, kk) + bias_ref[h]
            logits = jnp.where(valid, logits, NEG_INF)
            sink = sink_ref[h]
            m = jnp.maximum(jnp.max(logits, axis=-1, keepdims=True), sink)
            p = jnp.exp(logits - m)
            denom = jnp.sum(p, axis=-1, keepdims=True) + jnp.exp(sink - m)
            probs = p * (1.0 / denom)
            outs.append(_dot(probs.astype(_BF16), vv))
        o_ref[0, :, c * LANES:(c + 1) * LANES] = jnp.where(
            lane < HEAD_DIM, outs[0], outs[1]).astype(o_ref.dtype)


def _sw_attn(qkv, sinks, bias, *, q_col, k_col, v_col):
    b, s, _ = qkv.shape
    qw = SW_HEADS * HEAD_DIM
    kw = SW_KV_HEADS * LANES
    prev = lambda bi, n: (bi, jnp.maximum(n - 1, 0))
    return pl.pallas_call(
        _sw_kernel,
        grid=(b, s // BLOCK),
        in_specs=[
            pl.BlockSpec(memory_space=pltpu.SMEM),
            pl.BlockSpec((1, BLOCK, qw), lambda bi, n: (bi, n, q_col)),
            pl.BlockSpec((1, BLOCK, kw), lambda bi, n: prev(bi, n) + (k_col,)),
            pl.BlockSpec((1, BLOCK, kw), lambda bi, n: (bi, n, k_col)),
            pl.BlockSpec((1, BLOCK, kw), lambda bi, n: prev(bi, n) + (v_col,)),
            pl.BlockSpec((1, BLOCK, kw), lambda bi, n: (bi, n, v_col)),
            pl.BlockSpec((SW_HEADS, BLOCK, 2 * BLOCK), lambda bi, n: (0, 0, 0)),
        ],
        out_specs=pl.BlockSpec((1, BLOCK, qw), lambda bi, n: (bi, n, 0)),
        out_shape=jax.ShapeDtypeStruct((b, s, qw), _BF16),
        compiler_params=pltpu.CompilerParams(
            dimension_semantics=("arbitrary", "arbitrary"), vmem_limit_bytes=VMEM_LIMIT),
        name="sw_attn",
    )(sinks.astype(_F32), qkv, qkv, qkv, qkv, qkv, bias)


def _mix_kernel(x_ref, g_ref, wg_ref, ysb_ref, ysw_ref, wsb_ref, wsw_ref, wo_ref, o_ref):
    x = x_ref[...]
    d = x.shape[1]
    h = _rms(x, g_ref[...]).astype(_BF16)
    gates = jax.nn.sigmoid(_dot(h, wg_ref[...]))
    merged = (gates[:, :d] * _dot(ysb_ref[...], wsb_ref[...])
              + gates[:, d:] * _dot(ysw_ref[...], wsw_ref[...]))
    o_ref[...] = x + _dot(merged.astype(_BF16), wo_ref[...])


def _mix_out(x2d, g, w_gate, y_sb, y_sw, w_sb, w_sw, w_out, *, tm):
    n, d = x2d.shape
    full = lambda a: pl.BlockSpec(a.shape, lambda i: (0, 0))
    rows = lambda a: pl.BlockSpec((tm, a.shape[1]), lambda i: (i, 0))
    return pl.pallas_call(
        _mix_kernel,
        grid=(n // tm,),
        in_specs=[rows(x2d), full(g), full(w_gate), rows(y_sb), rows(y_sw),
                  full(w_sb), full(w_sw), full(w_out)],
        out_specs=pl.BlockSpec((tm, d), lambda i: (i, 0)),
        out_shape=jax.ShapeDtypeStruct((n, d), _F32),
        compiler_params=pltpu.CompilerParams(
            dimension_semantics=("arbitrary",), vmem_limit_bytes=VMEM_LIMIT),
        name="mix_out",
    )(x2d, g, w_gate, y_sb, y_sw, w_sb, w_sw, w_out)


HALO = 8


def _ffn_kernel(x_ref, g_ref, wu_ref, cw_ref, cb_ref, wd_ref, gf_ref, o_ref,
                h_scr, acc_scr, u_scr, carry_scr, *, tiles_per_seq):
    i = pl.program_id(0)
    j = pl.program_id(1)
    tm = x_ref.shape[0]

    @pl.when(j == 0)
    def _():
        h_scr[...] = _rms(x_ref[...], g_ref[...]).astype(_BF16)
        acc_scr[...] = jnp.zeros_like(acc_scr)

    @pl.when(i % tiles_per_seq == 0)
    def _():
        carry_scr[j] = jnp.zeros(carry_scr.shape[1:], _F32)

    u = _dot(h_scr[...], wu_ref[...])
    u_scr[HALO:HALO + tm, :] = u
    u_scr[0:HALO, :] = carry_scr[j]
    carry_scr[j] = u[tm - HALO:tm, :]
    y = cb_ref[...] + cw_ref[CONV_WIDTH - 1:CONV_WIDTH, :] * u
    for tap in range(CONV_WIDTH - 1):
        back = CONV_WIDTH - 1 - tap
        y = y + cw_ref[tap:tap + 1, :] * u_scr[HALO - back:HALO - back + tm, :]
    act = jax.nn.silu(y[:, :FF_CHUNK]) * y[:, FF_CHUNK:]
    acc_scr[...] += _dot(act.astype(_BF16), wd_ref[...])

    @pl.when(j == pl.num_programs(1) - 1)
    def _():
        o_ref[...] = _rms(x_ref[...] + acc_scr[...], gf_ref[...])


def _conv_ffn(x1, g_ffn, w_up, conv_w, conv_b, w_down, g_final, *, tm, tiles_per_seq):
    n, d = x1.shape
    d_ff = w_down.shape[0]
    nj = d_ff // FF_CHUNK
    return pl.pallas_call(
        functools.partial(_ffn_kernel, tiles_per_seq=tiles_per_seq),
        grid=(n // tm, nj),
        in_specs=[
            pl.BlockSpec((tm, d), lambda i, j: (i, 0)),
            pl.BlockSpec((1, d), lambda i, j: (0, 0)),
            pl.BlockSpec((d, 2 * FF_CHUNK), lambda i, j: (0, j)),
            pl.BlockSpec((CONV_WIDTH, 2 * FF_CHUNK), lambda i, j: (0, j)),
            pl.BlockSpec((1, 2 * FF_CHUNK), lambda i, j: (0, j)),
            pl.BlockSpec((FF_CHUNK, d), lambda i, j: (j, 0)),
            pl.BlockSpec((1, d), lambda i, j: (0, 0)),
        ],
        out_specs=pl.BlockSpec((tm, d), lambda i, j: (i, 0)),
        out_shape=jax.ShapeDtypeStruct((n, d), _F32),
        scratch_shapes=[
            pltpu.VMEM((tm, d), _BF16),
            pltpu.VMEM((tm, d), _F32),
            pltpu.VMEM((HALO + tm, 2 * FF_CHUNK), _F32),
            pltpu.VMEM((nj, HALO, 2 * FF_CHUNK), _F32),
        ],
        compiler_params=pltpu.CompilerParams(
            dimension_semantics=("arbitrary", "arbitrary"), vmem_limit_bytes=VMEM_LIMIT),
        name="conv_ffn",
    )(x1, g_ffn, w_up, conv_w, conv_b, w_down, g_final)


def _interleave_ff(a, d_ff):
    lead = a.shape[:-1]
    a = a.reshape(lead + (2, d_ff // FF_CHUNK, FF_CHUNK))
    return jnp.swapaxes(a, -3, -2).reshape(lead + (2 * d_ff,))


def kernel(x, g_mix, w_in, w_sb_proj, w_sw_proj, w_out, rel_bias, sinks, g_ffn, w_up,
           conv_w, conv_b, w_down, g_final):
    b, s, d = x.shape
    n = b * s
    d_ff = w_down.shape[0]
    sb_w = SB_HEADS * HEAD_DIM
    sw_qw = SW_HEADS * HEAD_DIM
    sw_kw = SW_KV_HEADS * HEAD_DIM
    assert s % BLOCK == 0 and d_ff % FF_CHUNK == 0
    tm = min(512, s)
    assert s % tm == 0

    qkv_end = 3 * sb_w + sw_qw
    k_sw = w_in[:, qkv_end:qkv_end + sw_kw].reshape(d, SW_KV_HEADS, 1, HEAD_DIM)
    v_sw = w_in[:, qkv_end + sw_kw:qkv_end + 2 * sw_kw].reshape(d, SW_KV_HEADS, 1, HEAD_DIM)
    dup = lambda a: jnp.broadcast_to(
        a, (d, SW_KV_HEADS, LANES // HEAD_DIM, HEAD_DIM)).reshape(d, SW_KV_HEADS * LANES)
    w_qkv = jnp.concatenate([w_in[:, :qkv_end], dup(k_sw), dup(v_sw)], axis=1).astype(_BF16)
    w_gate = w_in[:, qkv_end + 2 * sw_kw:].astype(_BF16)
    row = lambda a: a.reshape(1, -1).astype(_F32)

    x2d = x.reshape(n, d)
    chunk = 512
    qkv = _in_proj(x2d, row(g_mix), w_qkv, tm=tm, chunk=chunk,
                   n_scaled=(0, 3 * sb_w // chunk))
    qkv = qkv.reshape(b, s, -1)
    y_sb = _sb_attn(qkv, q_col=0, k_col=sb_w // LANES, v_col=2 * sb_w // LANES)
    bias = _sw_bias(rel_bias)
    kv_tile = SW_KV_HEADS * LANES
    y_sw = _sw_attn(qkv, sinks, bias, q_col=3 * sb_w // sw_qw,
                    k_col=qkv_end // kv_tile, v_col=qkv_end // kv_tile + 1)
    x1 = _mix_out(x2d, row(g_mix), w_gate, y_sb.reshape(n, sb_w), y_sw.reshape(n, sw_qw),
                  w_sb_proj.astype(_BF16), w_sw_proj.astype(_BF16), w_out.astype(_BF16), tm=tm)
    out = _conv_ffn(x1, row(g_ffn), _interleave_ff(w_up, d_ff).astype(_BF16),
                    _interleave_ff(conv_w, d_ff).astype(_F32),
                    _interleave_ff(conv_b, d_ff).reshape(1, -1).astype(_F32),
                    w_down.astype(_BF16), row(g_final), tm=tm, tiles_per_seq=s // tm)
    return out.reshape(b, s, d)
```

```python
import functools
import math

import jax
import jax.numpy as jnp
import numpy as np
from jax import lax
from jax.experimental import pallas as pl
from jax.experimental.pallas import tpu as pltpu

SB_HEADS = 8
SW_HEADS = 8
SW_KV_HEADS = 2
HEAD_DIM = 64
BLOCK = 128
NUM_BUCKETS = 32
MAX_DISTANCE = 128
CONV_WIDTH = 3
EPS = 1e-6
NEG_INF = -1e30
LANES = 128
FF_CHUNK = 256
VMEM_LIMIT = 56 * 1024 * 1024

_BF16 = jnp.bfloat16
_F32 = jnp.float32


def _rms(x, g):
    return x * lax.rsqrt(jnp.mean(x * x, axis=-1, keepdims=True) + EPS) * g


def _dot(a, b):
    return jnp.dot(a, b, preferred_element_type=_F32)


def _dot_nt(a, b):
    return lax.dot_general(a, b, (((1,), (1,)), ((), ())), preferred_element_type=_F32)


def _in_proj_kernel(x_ref, g_ref, w_ref, o_ref, *, n_scaled, chunk):
    h = _rms(x_ref[...], g_ref[...]).astype(_BF16)
    scale = 1.0 / math.sqrt(HEAD_DIM)
    for c in range(w_ref.shape[1] // chunk):
        cols = slice(c * chunk, (c + 1) * chunk)
        p = _dot(h, w_ref[:, cols])
        if c in n_scaled:
            p = p * scale
        o_ref[:, cols] = p.astype(_BF16)


def _in_proj(x2d, g, w, *, tm, n_scaled, chunk):
    n, d = x2d.shape
    width = w.shape[1]
    return pl.pallas_call(
        functools.partial(_in_proj_kernel, n_scaled=n_scaled, chunk=chunk),
        grid=(n // tm,),
        in_specs=[
            pl.BlockSpec((tm, d), lambda i: (i, 0)),
            pl.BlockSpec((1, d), lambda i: (0, 0)),
            pl.BlockSpec((d, width), lambda i: (0, 0)),
        ],
        out_specs=pl.BlockSpec((tm, width), lambda i: (i, 0)),
        out_shape=jax.ShapeDtypeStruct((n, width), _BF16),
        compiler_params=pltpu.CompilerParams(
            dimension_semantics=("arbitrary",), vmem_limit_bytes=VMEM_LIMIT),
        name="in_proj",
    )(x2d, g, w)


SB_SUB = 8
SB_FIXED = 3
SB_OFF = 1e30
LOG2E = 1.4426950408889634
SB_DEAD_LOG2 = 151.0


def _sb_kernel(q_ref, k_ref, v_ref, t_ref, o_ref, rem_scr, acc_scr):
    base = pl.program_id(2) * SB_SUB
    tri = t_ref[...]
    rows2 = 2 * BLOCK
    lane = lax.broadcasted_iota(jnp.int32, (rows2, LANES), 1)
    row2 = lax.broadcasted_iota(jnp.int32, (rows2, LANES), 0)
    own_lanes = (lane >= HEAD_DIM) == (row2 >= BLOCK)
    strictly_earlier = lane < (row2 & (BLOCK - 1))

    def stacked_q(a):
        q = q_ref[0, a * BLOCK:(a + 1) * BLOCK, :]
        q2 = jnp.concatenate([q, q], axis=0)
        return jnp.where(own_lanes, q2, jnp.zeros_like(q2))

    def key_rows(kb):
        return pl.ds(pl.multiple_of(kb * BLOCK, BLOCK), BLOCK)

    def scores(q2, kb):
        return _dot_nt(q2, k_ref[0, key_rows(kb), :]) * LOG2E

    def suffix_mass(z, diagonal):
        sp = jnp.maximum(z, 0.0) + jnp.log2(1.0 + jnp.exp2(-jnp.abs(z)))
        if diagonal:
            sp = jnp.where(strictly_earlier, sp, 0.0)
        hi = sp.astype(_BF16)
        lo = (sp - hi.astype(_F32)).astype(_BF16)
        return _dot(jnp.concatenate([hi, lo], axis=1), tri)

    def weighted_values(z, cum, rem, kb, diagonal):
        w = jnp.exp2(z - cum[:, :BLOCK] - rem)
        if diagonal:
            w = jnp.where(strictly_earlier, w, 0.0)
        return _dot(w.astype(_BF16), v_ref[0, key_rows(kb), :])

    def visit(q2, kb, rem, acc, diagonal):
        z = scores(q2, kb)
        cum = suffix_mass(z, diagonal)
        return rem + cum[:, BLOCK:], acc + weighted_values(z, cum, rem, kb, diagonal)

    def earlier_block(a, t, rem):
        kb = base + a - t
        return jnp.maximum(kb, 0), rem + jnp.where(kb < 0, SB_OFF, 0.0)

    items = [(a, t) for a in range(SB_SUB) for t in range(SB_FIXED)]
    q2s, zs, cums = {}, {}, {}
    rems = [jnp.zeros((rows2, BLOCK), _F32) for _ in range(SB_SUB)]
    accs = [jnp.zeros((rows2, LANES), _F32) for _ in range(SB_SUB)]
    for step in range(len(items) + 2):
        if step < len(items):
            a, t = items[step]
            if t == 0:
                q2s[a] = stacked_q(a)
            zs[step] = scores(q2s[a], jnp.maximum(base + a - t, 0))
        if 0 <= step - 1 < len(items):
            cums[step - 1] = suffix_mass(zs[step - 1], items[step - 1][1] == 0)
        if 0 <= step - 2 < len(items):
            a, t = items[step - 2]
            kb, rem = earlier_block(a, t, rems[a])
            accs[a] = accs[a] + weighted_values(zs.pop(step - 2), cums[step - 2], rem, kb, t == 0)
            rems[a] = rem + cums.pop(step - 2)[:, BLOCK:]
    floor = None
    for a in range(SB_SUB):
        rem_scr[a] = rems[a]
        acc_scr[a] = accs[a]
        floor = rems[a] if floor is None else jnp.minimum(floor, rems[a])

    def cond(carry):
        t, least = carry
        return (base + SB_SUB - 1 - t >= 0) & (least < SB_DEAD_LOG2)

    def body(carry):
        t, _ = carry
        floor = None
        for a in range(SB_SUB):
            kb, rem = earlier_block(a, t, rem_scr[a])
            rem, acc = visit(stacked_q(a), kb, rem, acc_scr[a], False)
            rem_scr[a] = rem
            acc_scr[a] = acc
            floor = rem if floor is None else jnp.minimum(floor, rem)
        return t + 1, jnp.min(floor)

    lax.while_loop(cond, body, (jnp.int32(SB_FIXED), jnp.min(floor)))
    head0_lanes = lax.broadcasted_iota(jnp.int32, (BLOCK, LANES), 1) < HEAD_DIM
    for a in range(SB_SUB):
        o_ref[0, a * BLOCK:(a + 1) * BLOCK, :] = jnp.where(
            head0_lanes, acc_scr[a, :BLOCK, :], acc_scr[a, BLOCK:, :]).astype(o_ref.dtype)


def _suffix_sum_matrix():
    j = np.arange(2 * BLOCK)[:, None] % BLOCK
    s = np.arange(2 * BLOCK)[None, :]
    return jnp.asarray(((s >= BLOCK) | (j >= s)).astype(np.float32), dtype=_BF16)


def _sb_attn(qkv, *, q_col, k_col, v_col):
    b, s, _ = qkv.shape
    pairs = SB_HEADS * HEAD_DIM // LANES
    tq = SB_SUB * BLOCK
    assert s % tq == 0
    return pl.pallas_call(
        _sb_kernel,
        grid=(b, pairs, s // tq),
        in_specs=[
            pl.BlockSpec((1, tq, LANES), lambda bi, p, qi: (bi, qi, q_col + p)),
            pl.BlockSpec((1, s, LANES), lambda bi, p, qi: (bi, 0, k_col + p)),
            pl.BlockSpec((1, s, LANES), lambda bi, p, qi: (bi, 0, v_col + p)),
            pl.BlockSpec((2 * BLOCK, 2 * BLOCK), lambda bi, p, qi: (0, 0)),
        ],
        out_specs=pl.BlockSpec((1, tq, LANES), lambda bi, p, qi: (bi, qi, p)),
        out_shape=jax.ShapeDtypeStruct((b, s, SB_HEADS * HEAD_DIM), _BF16),
        scratch_shapes=[
            pltpu.VMEM((SB_SUB, 2 * BLOCK, BLOCK), _F32),
            pltpu.VMEM((SB_SUB, 2 * BLOCK, LANES), _F32),
        ],
        compiler_params=pltpu.CompilerParams(
            dimension_semantics=("arbitrary", "arbitrary", "arbitrary"),
            vmem_limit_bytes=VMEM_LIMIT),
        name="sb_attn",
    )(qkv, qkv, qkv, _suffix_sum_matrix())


def _sw_bias_kernel(bucket_ref, rel_ref, o_ref):
    bucket = bucket_ref[...]
    for h in range(SW_HEADS):
        acc = jnp.zeros(bucket.shape, _F32)
        for b in range(NUM_BUCKETS):
            acc = jnp.where(bucket == b, rel_ref[b, h], acc)
        o_ref[h] = acc


def _t5_bucket(dist):
    max_exact = NUM_BUCKETS // 2
    d = jnp.maximum(dist, 1).astype(_F32)
    large = max_exact + (jnp.log(d / max_exact) / math.log(MAX_DISTANCE / max_exact)
                         * (NUM_BUCKETS - max_exact)).astype(jnp.int32)
    large = jnp.minimum(large, NUM_BUCKETS - 1)
    return jnp.where(dist < max_exact, dist, large)


def _sw_bias(rel_bias):
    dist = (jnp.arange(BLOCK)[:, None] + BLOCK) - jnp.arange(2 * BLOCK)[None, :]
    bucket = _t5_bucket(jnp.maximum(dist, 0)).astype(jnp.int32)
    return pl.pallas_call(
        _sw_bias_kernel,
        in_specs=[
            pl.BlockSpec(memory_space=pltpu.VMEM),
            pl.BlockSpec(memory_space=pltpu.SMEM),
        ],
        out_specs=pl.BlockSpec(memory_space=pltpu.VMEM),
        out_shape=jax.ShapeDtypeStruct((SW_HEADS, BLOCK, 2 * BLOCK), _F32),
        name="sw_bias",
    )(bucket, rel_bias.astype(_F32))


SW_SUB = 4
SW_GROUP = SW_HEADS // SW_KV_HEADS


def _sw_kernel(q_ref, kp_ref, kc_ref, vp_ref, vc_ref, bias_ref, o_ref):
    rows = SW_GROUP * BLOCK
    half = rows // 2
    lane = lax.broadcasted_iota(jnp.int32, (rows, LANES), 1)
    odd_rows = lax.broadcasted_iota(jnp.int32, (rows, LANES), 0) >= half
    own_lanes = (lane >= HEAD_DIM) == odd_rows
    sink_key = lax.broadcasted_iota(jnp.int32, (2 * BLOCK, LANES), 0) == 0
    qpos = (lax.broadcasted_iota(jnp.int32, (rows, 2 * BLOCK), 0) & (BLOCK - 1)) + BLOCK
    kpos = lax.broadcasted_iota(jnp.int32, (rows, 2 * BLOCK), 1)
    dist = qpos - kpos
    in_window = (dist >= 0) & (dist < BLOCK)
    first_key = jnp.where(pl.program_id(1) > 0, 0, BLOCK)
    valid_first = (in_window & (kpos >= first_key)) | (kpos == 0)
    valid_later = in_window | (kpos == 0)
    head0_lanes = lax.broadcasted_iota(jnp.int32, (BLOCK, LANES), 1) < HEAD_DIM
    tiles_per_group = SW_GROUP * HEAD_DIM // LANES

    def kv_rows(prev_ref, cur_ref, blk, g):
        cols = slice(g * LANES, (g + 1) * LANES)
        cur = cur_ref[0, blk * BLOCK:(blk + 1) * BLOCK, cols]
        prev = (prev_ref[0, :, cols] if blk == 0
                else cur_ref[0, (blk - 1) * BLOCK:blk * BLOCK, cols])
        kv = jnp.concatenate([prev, cur], axis=0)
        return jnp.where(sink_key, jnp.zeros_like(kv), kv)

    def logits(blk, g):
        tiles = [q_ref[0, blk * BLOCK:(blk + 1) * BLOCK, c * LANES:(c + 1) * LANES]
                 for c in range(g * tiles_per_group, (g + 1) * tiles_per_group)]
        q4 = jnp.concatenate(tiles + tiles, axis=0)
        q4 = jnp.where(own_lanes, q4, jnp.zeros_like(q4))
        return _dot_nt(q4, kv_rows(kp_ref, kc_ref, blk, g))

    def exponentials(z, blk, g):
        z = jnp.where(valid_first if blk == 0 else valid_later, z + bias_ref[g], NEG_INF)
        return jnp.exp(z - jnp.max(z, axis=-1, keepdims=True)).astype(_BF16)

    def values(p, blk, g):
        vv = kv_rows(vp_ref, vc_ref, blk, g)
        pv = _dot(p, jnp.concatenate([vv, jnp.ones_like(vv)], axis=1))
        out = pv[:, :LANES] * (1.0 / pv[:, LANES:])
        for tile in range(tiles_per_group):
            c = g * tiles_per_group + tile
            o_ref[0, blk * BLOCK:(blk + 1) * BLOCK, c * LANES:(c + 1) * LANES] = jnp.where(
                head0_lanes, out[tile * BLOCK:(tile + 1) * BLOCK],
                out[half + tile * BLOCK:half + (tile + 1) * BLOCK]).astype(o_ref.dtype)

    items = [(blk, g) for blk in range(SW_SUB) for g in range(SW_KV_HEADS)]
    zs, ps = {}, {}
    for step in range(len(items) + 2):
        if step < len(items):
            zs[step] = logits(*items[step])
        if 0 <= step - 1 < len(items):
            ps[step - 1] = exponentials(zs.pop(step - 1), *items[step - 1])
        if 0 <= step - 2 < len(items):
            values(ps.pop(step - 2), *items[step - 2])


def _sw_attn(qkv, sinks, bias, *, q_col, k_col, v_col):
    b, s, _ = qkv.shape
    qw = SW_HEADS * HEAD_DIM
    kw = SW_KV_HEADS * LANES
    tq = SW_SUB * BLOCK
    assert s % tq == 0 and LANES == 2 * HEAD_DIM
    prev = lambda bi, n: (bi, jnp.maximum(n * SW_SUB - 1, 0))
    tiles = SW_GROUP * HEAD_DIM // LANES
    table = bias.at[:, :, 0].set(sinks.astype(_F32)[:, None])
    table = table.reshape(SW_KV_HEADS, tiles, 2, BLOCK, 2 * BLOCK).swapaxes(1, 2).reshape(
        SW_KV_HEADS, SW_GROUP * BLOCK, 2 * BLOCK)
    return pl.pallas_call(
        _sw_kernel,
        grid=(b, s // tq),
        in_specs=[
            pl.BlockSpec((1, tq, qw), lambda bi, n: (bi, n, q_col)),
            pl.BlockSpec((1, BLOCK, kw), lambda bi, n: prev(bi, n) + (k_col,)),
            pl.BlockSpec((1, tq, kw), lambda bi, n: (bi, n, k_col)),
            pl.BlockSpec((1, BLOCK, kw), lambda bi, n: prev(bi, n) + (v_col,)),
            pl.BlockSpec((1, tq, kw), lambda bi, n: (bi, n, v_col)),
            pl.BlockSpec((SW_KV_HEADS, SW_GROUP * BLOCK, 2 * BLOCK), lambda bi, n: (0, 0, 0)),
        ],
        out_specs=pl.BlockSpec((1, tq, qw), lambda bi, n: (bi, n, 0)),
        out_shape=jax.ShapeDtypeStruct((b, s, qw), _BF16),
        compiler_params=pltpu.CompilerParams(
            dimension_semantics=("arbitrary", "arbitrary"), vmem_limit_bytes=VMEM_LIMIT),
        name="sw_attn",
    )(qkv, qkv, qkv, qkv, qkv, table)


MIX_CHUNK = 256


def _mix_kernel(x_ref, g_ref, wg_ref, ysb_ref, ysw_ref, wsb_ref, wsw_ref, wo_ref, o_ref,
                h_scr, merged_scr):
    d = x_ref.shape[1]
    h_scr[...] = _rms(x_ref[...], g_ref[...]).astype(_BF16)

    def projections(c):
        cols = slice(c * MIX_CHUNK, (c + 1) * MIX_CHUNK)
        gate_cols = slice(d + c * MIX_CHUNK, d + (c + 1) * MIX_CHUNK)
        h = h_scr[...]
        return (_dot(h, wg_ref[:, cols]), _dot(ysb_ref[...], wsb_ref[:, cols]),
                _dot(h, wg_ref[:, gate_cols]), _dot(ysw_ref[...], wsw_ref[:, cols]))

    n_chunks = d // MIX_CHUNK
    ahead = projections(0)
    for c in range(n_chunks):
        logit_sb, y_sb, logit_sw, y_sw = ahead
        if c + 1 < n_chunks:
            ahead = projections(c + 1)
        merged = jax.nn.sigmoid(logit_sb) * y_sb + jax.nn.sigmoid(logit_sw) * y_sw
        merged_scr[:, c * MIX_CHUNK:(c + 1) * MIX_CHUNK] = merged.astype(_BF16)
    o_ref[...] = x_ref[...] + _dot(merged_scr[...], wo_ref[...])


def _mix_out(x2d, g, w_gate, y_sb, y_sw, w_sb, w_sw, w_out, *, tm):
    n, d = x2d.shape
    assert d % MIX_CHUNK == 0
    full = lambda a: pl.BlockSpec(a.shape, lambda i: (0, 0), pipeline_mode=pl.Buffered(1))
    rows = lambda a: pl.BlockSpec((tm, a.shape[1]), lambda i: (i, 0))
    return pl.pallas_call(
        _mix_kernel,
        grid=(n // tm,),
        in_specs=[rows(x2d), full(g), full(w_gate), rows(y_sb), rows(y_sw),
                  full(w_sb), full(w_sw), full(w_out)],
        out_specs=pl.BlockSpec((tm, d), lambda i: (i, 0)),
        out_shape=jax.ShapeDtypeStruct((n, d), _F32),
        scratch_shapes=[pltpu.VMEM((tm, d), _BF16), pltpu.VMEM((tm, d), _BF16)],
        compiler_params=pltpu.CompilerParams(
            dimension_semantics=("arbitrary",), vmem_limit_bytes=VMEM_LIMIT),
        name="mix_out",
    )(x2d, g, w_gate, y_sb, y_sw, w_sb, w_sw, w_out)


HALO = 8


def _ffn_kernel(x_ref, g_ref, wu_ref, cw_ref, cb_ref, wd_ref, gf_ref, o_ref,
                h_scr, act_scr, u_scr, carry_scr, *, tiles_per_seq):
    tm = x_ref.shape[0]
    d_ff = wd_ref.shape[0]
    nj = d_ff // FF_CHUNK
    h_scr[...] = _rms(x_ref[...], g_ref[...]).astype(_BF16)

    @pl.when(pl.program_id(0) % tiles_per_seq == 0)
    def _():
        carry_scr[...] = jnp.zeros_like(carry_scr)

    def up_proj(j):
        h = h_scr[...]
        gate = _dot(h, wu_ref[:, j * FF_CHUNK:(j + 1) * FF_CHUNK])
        value = _dot(h, wu_ref[:, d_ff + j * FF_CHUNK:d_ff + (j + 1) * FF_CHUNK])
        return gate, value

    def gated_conv(j, gate, value):
        slot = j % 2
        outs = []
        for part, u in enumerate((gate, value)):
            cols = slice(part * d_ff + j * FF_CHUNK, part * d_ff + (j + 1) * FF_CHUNK)
            lanes = slice(part * FF_CHUNK, (part + 1) * FF_CHUNK)
            u_scr[slot, HALO:HALO + tm, lanes] = u
            u_scr[slot, 0:HALO, lanes] = carry_scr[j, :, lanes]
            carry_scr[j, :, lanes] = u[tm - HALO:tm, :]
            y = cb_ref[:, cols] + cw_ref[CONV_WIDTH - 1:CONV_WIDTH, cols] * u
            for tap in range(CONV_WIDTH - 1):
                back = CONV_WIDTH - 1 - tap
                y = y + cw_ref[tap:tap + 1, cols] * u_scr[slot, HALO - back:HALO - back + tm, lanes]
            outs.append(y)
        act = jax.nn.silu(outs[0]) * outs[1]
        act_scr[:, j * FF_CHUNK:(j + 1) * FF_CHUNK] = act.astype(_BF16)

    ahead = up_proj(0)
    for j in range(nj):
        gate, value = ahead
        if j + 1 < nj:
            ahead = up_proj(j + 1)
        gated_conv(j, gate, value)
    o_ref[...] = _rms(x_ref[...] + _dot(act_scr[...], wd_ref[...]), gf_ref[...])


def _conv_ffn(x1, g_ffn, w_up, conv_w, conv_b, w_down, g_final, *, tm, tiles_per_seq):
    n, d = x1.shape
    d_ff = w_down.shape[0]
    nj = d_ff // FF_CHUNK
    resident = lambda a: pl.BlockSpec(a.shape, lambda i: (0, 0), pipeline_mode=pl.Buffered(1))
    return pl.pallas_call(
        functools.partial(_ffn_kernel, tiles_per_seq=tiles_per_seq),
        grid=(n // tm,),
        in_specs=[
            pl.BlockSpec((tm, d), lambda i: (i, 0)),
            resident(g_ffn), resident(w_up), resident(conv_w), resident(conv_b),
            resident(w_down), resident(g_final),
        ],
        out_specs=pl.BlockSpec((tm, d), lambda i: (i, 0)),
        out_shape=jax.ShapeDtypeStruct((n, d), _F32),
        scratch_shapes=[
            pltpu.VMEM((tm, d), _BF16),
            pltpu.VMEM((tm, d_ff), _BF16),
            pltpu.VMEM((2, HALO + tm, 2 * FF_CHUNK), _F32),
            pltpu.VMEM((nj, HALO, 2 * FF_CHUNK), _F32),
        ],
        compiler_params=pltpu.CompilerParams(
            dimension_semantics=("arbitrary",), vmem_limit_bytes=VMEM_LIMIT),
        name="conv_ffn",
    )(x1, g_ffn, w_up, conv_w, conv_b, w_down, g_final)


def kernel(x, g_mix, w_in, w_sb_proj, w_sw_proj, w_out, rel_bias, sinks, g_ffn, w_up,
           conv_w, conv_b, w_down, g_final):
    b, s, d = x.shape
    n = b * s
    d_ff = w_down.shape[0]
    sb_w = SB_HEADS * HEAD_DIM
    sw_qw = SW_HEADS * HEAD_DIM
    sw_kw = SW_KV_HEADS * HEAD_DIM
    assert s % BLOCK == 0 and d_ff % FF_CHUNK == 0
    tm = min(512, s)
    assert s % tm == 0

    qkv_end = 3 * sb_w + sw_qw
    k_sw = w_in[:, qkv_end:qkv_end + sw_kw].reshape(d, SW_KV_HEADS, 1, HEAD_DIM)
    v_sw = w_in[:, qkv_end + sw_kw:qkv_end + 2 * sw_kw].reshape(d, SW_KV_HEADS, 1, HEAD_DIM)
    dup = lambda a: jnp.broadcast_to(
        a, (d, SW_KV_HEADS, LANES // HEAD_DIM, HEAD_DIM)).reshape(d, SW_KV_HEADS * LANES)
    w_qkv = jnp.concatenate([w_in[:, :qkv_end], dup(k_sw), dup(v_sw)], axis=1).astype(_BF16)
    w_gate = w_in[:, qkv_end + 2 * sw_kw:].astype(_BF16)
    row = lambda a: a.reshape(1, -1).astype(_F32)

    x2d = x.reshape(n, d)
    chunk = 512
    qkv = _in_proj(x2d, row(g_mix), w_qkv, tm=tm, chunk=chunk,
                   n_scaled=(0, 3 * sb_w // chunk))
    qkv = qkv.reshape(b, s, -1)
    y_sb = _sb_attn(qkv, q_col=0, k_col=sb_w // LANES, v_col=2 * sb_w // LANES)
    bias = _sw_bias(rel_bias)
    kv_tile = SW_KV_HEADS * LANES
    y_sw = _sw_attn(qkv, sinks, bias, q_col=3 * sb_w // sw_qw,
                    k_col=qkv_end // kv_tile, v_col=qkv_end // kv_tile + 1)
    x1 = _mix_out(x2d, row(g_mix), w_gate, y_sb.reshape(n, sb_w), y_sw.reshape(n, sw_qw),
                  w_sb_proj.astype(_BF16), w_sw_proj.astype(_BF16), w_out.astype(_BF16), tm=tm)
    out = _conv_ffn(x1, row(g_ffn), w_up.astype(_BF16), conv_w.astype(_F32), row(conv_b),
                    w_down.astype(_BF16), row(g_final), tm=tm, tiles_per_seq=s // tm)
    return out.reshape(b, s, d)
```

```python
import functools
import math

import jax
import jax.numpy as jnp
import numpy as np
from jax import lax
from jax.experimental import pallas as pl
from jax.experimental.pallas import tpu as pltpu

SB_HEADS = 8
SW_HEADS = 8
SW_KV_HEADS = 2
HEAD_DIM = 64
BLOCK = 128
NUM_BUCKETS = 32
MAX_DISTANCE = 128
CONV_WIDTH = 3
EPS = 1e-6
NEG_INF = -1e30
LANES = 128
FF_CHUNK = 256
VMEM_LIMIT = 56 * 1024 * 1024

_BF16 = jnp.bfloat16
_F32 = jnp.float32


def _rms(x, g):
    return x * lax.rsqrt(jnp.mean(x * x, axis=-1, keepdims=True) + EPS) * g


def _dot(a, b):
    return jnp.dot(a, b, preferred_element_type=_F32)


def _dot_nt(a, b):
    return lax.dot_general(a, b, (((1,), (1,)), ((), ())), preferred_element_type=_F32)


def _in_proj_kernel(x_ref, g_ref, w_ref, o_ref, *, n_scaled, chunk):
    h = _rms(x_ref[...], g_ref[...]).astype(_BF16)
    scale = 1.0 / math.sqrt(HEAD_DIM)
    for c in range(w_ref.shape[1] // chunk):
        cols = slice(c * chunk, (c + 1) * chunk)
        p = _dot(h, w_ref[:, cols])
        if c in n_scaled:
            p = p * scale
        o_ref[:, cols] = p.astype(_BF16)


def _in_proj(x2d, g, w, *, tm, n_scaled, chunk):
    n, d = x2d.shape
    width = w.shape[1]
    return pl.pallas_call(
        functools.partial(_in_proj_kernel, n_scaled=n_scaled, chunk=chunk),
        grid=(n // tm,),
        in_specs=[
            pl.BlockSpec((tm, d), lambda i: (i, 0)),
            pl.BlockSpec((1, d), lambda i: (0, 0)),
            pl.BlockSpec((d, width), lambda i: (0, 0)),
        ],
        out_specs=pl.BlockSpec((tm, width), lambda i: (i, 0)),
        out_shape=jax.ShapeDtypeStruct((n, width), _BF16),
        compiler_params=pltpu.CompilerParams(
            dimension_semantics=("arbitrary",), vmem_limit_bytes=VMEM_LIMIT),
        name="in_proj",
    )(x2d, g, w)


SB_SUB = 8
SB_TOP = 32
SB_SKEW = 3
SB_OFF = 1e30
LOG2E = 1.4426950408889634
SB_DEAD_LOG2 = 151.0


def _sb_kernel(q_ref, k_ref, v_ref, t_ref, o_ref, rem_scr, acc_scr):
    base = pl.program_id(2) * SB_SUB
    tri = t_ref[...]
    rows2, top2 = 2 * BLOCK, 2 * SB_TOP
    rest = BLOCK - SB_TOP
    lane = lax.broadcasted_iota(jnp.int32, (rows2, LANES), 1)
    row = lax.broadcasted_iota(jnp.int32, (rows2, LANES), 0)
    head1 = ((row >= SB_TOP) & (row < top2)) | (row >= top2 + rest)
    own_lanes = (lane >= HEAD_DIM) == head1
    query = jnp.where(row < SB_TOP, row, jnp.where(row < top2 + rest, row - SB_TOP, row - BLOCK))
    strictly_earlier = lane < query

    def stacked_q(a):
        q = q_ref[0, a * BLOCK:(a + 1) * BLOCK, :]
        q2 = jnp.concatenate([q[:SB_TOP], q[:SB_TOP], q[SB_TOP:], q[SB_TOP:]], axis=0)
        return jnp.where(own_lanes, q2, jnp.zeros_like(q2))

    def key_rows(kb):
        return pl.ds(pl.multiple_of(kb * BLOCK, BLOCK), BLOCK)

    def scores(q2, kb):
        return _dot_nt(q2, k_ref[0, key_rows(kb), :]) * LOG2E

    def suffix_mass(z, diagonal):
        sp = jnp.maximum(z, 0.0) + jnp.log2(1.0 + jnp.exp2(-jnp.abs(z)))
        if diagonal:
            sp = jnp.where(strictly_earlier, sp, 0.0)
        hi = sp.astype(_BF16)
        lo = (sp - hi.astype(_F32)).astype(_BF16)
        return _dot(jnp.concatenate([hi, lo], axis=1), tri)

    def weighted_values(z, cum, rem, kb, diagonal):
        w = jnp.exp2(z - cum[:, :BLOCK] - rem)
        if diagonal:
            w = jnp.where(strictly_earlier, w, 0.0)
        return _dot(w.astype(_BF16), v_ref[0, key_rows(kb), :])

    def earlier_block(a, t, rem):
        kb = base + a - t
        return jnp.maximum(kb, 0), rem + jnp.where(kb < 0, SB_OFF, 0.0)

    items = [(a, t, slice(0, top2) if t == 2 else slice(0, rows2))
             for a in range(SB_SUB) for t in range(3)]
    q2s, zs, cums = {}, {}, {}
    rems = [jnp.zeros((rows2, BLOCK), _F32) for _ in range(SB_SUB)]
    accs = [jnp.zeros((rows2, LANES), _F32) for _ in range(SB_SUB)]
    for step in range(len(items) + 2 * SB_SKEW):
        if step < len(items):
            a, t, part = items[step]
            if t == 0:
                q2s[a] = stacked_q(a)
            zs[step] = scores(q2s[a][part], jnp.maximum(base + a - t, 0))
        i = step - SB_SKEW
        if 0 <= i < len(items):
            cums[i] = suffix_mass(zs[i], items[i][1] == 0)
        i = step - 2 * SB_SKEW
        if 0 <= i < len(items):
            a, t, part = items[i]
            kb, rem = earlier_block(a, t, rems[a][part])
            pv = weighted_values(zs.pop(i), cums[i], rem, kb, t == 0)
            rem = rem + cums.pop(i)[:, BLOCK:]
            acc = accs[a][part] + pv
            if part.stop < rows2:
                rem = jnp.concatenate([rem, rems[a][part.stop:]], axis=0)
                acc = jnp.concatenate([acc, accs[a][part.stop:]], axis=0)
            rems[a], accs[a] = rem, acc
    floor = None
    for a in range(SB_SUB):
        rem_scr[a] = rems[a]
        acc_scr[a] = accs[a]
        floor = rems[a] if floor is None else jnp.minimum(floor, rems[a])

    def cond(carry):
        t, least = carry
        return (base + SB_SUB - 1 - t >= 0) & (least < SB_DEAD_LOG2)

    def body(carry):
        t, _ = carry
        floor = None
        for a in range(SB_SUB):
            q2 = stacked_q(a)
            for part, back in ((slice(top2, rows2), t), (slice(0, top2), t + 1)):
                kb, rem = earlier_block(a, back, rem_scr[a, part, :])
                z = scores(q2[part], kb)
                cum = suffix_mass(z, False)
                acc_scr[a, part, :] += weighted_values(z, cum, rem, kb, False)
                rem = rem + cum[:, BLOCK:]
                rem_scr[a, part, :] = rem
                least = jnp.min(rem, axis=0, keepdims=True)
                floor = least if floor is None else jnp.minimum(floor, least)
        return t + 1, jnp.min(floor)

    lax.while_loop(cond, body, (jnp.int32(2), jnp.min(floor)))
    head0_lanes = lax.broadcasted_iota(jnp.int32, (BLOCK, LANES), 1) < HEAD_DIM
    for a in range(SB_SUB):
        head0 = jnp.concatenate([acc_scr[a, :SB_TOP, :], acc_scr[a, top2:top2 + rest, :]], axis=0)
        head1 = jnp.concatenate([acc_scr[a, SB_TOP:top2, :], acc_scr[a, top2 + rest:, :]], axis=0)
        o_ref[0, a * BLOCK:(a + 1) * BLOCK, :] = jnp.where(
            head0_lanes, head0, head1).astype(o_ref.dtype)


def _suffix_sum_matrix():
    j = np.arange(2 * BLOCK)[:, None] % BLOCK
    s = np.arange(2 * BLOCK)[None, :]
    return jnp.asarray(((s >= BLOCK) | (j >= s)).astype(np.float32), dtype=_BF16)


def _sb_attn(qkv, *, q_col, k_col, v_col):
    b, s, _ = qkv.shape
    pairs = SB_HEADS * HEAD_DIM // LANES
    tq = SB_SUB * BLOCK
    assert s % tq == 0
    return pl.pallas_call(
        _sb_kernel,
        grid=(b, pairs, s // tq),
        in_specs=[
            pl.BlockSpec((1, tq, LANES), lambda bi, p, qi: (bi, qi, q_col + p)),
            pl.BlockSpec((1, s, LANES), lambda bi, p, qi: (bi, 0, k_col + p)),
            pl.BlockSpec((1, s, LANES), lambda bi, p, qi: (bi, 0, v_col + p)),
            pl.BlockSpec((2 * BLOCK, 2 * BLOCK), lambda bi, p, qi: (0, 0)),
        ],
        out_specs=pl.BlockSpec((1, tq, LANES), lambda bi, p, qi: (bi, qi, p)),
        out_shape=jax.ShapeDtypeStruct((b, s, SB_HEADS * HEAD_DIM), _BF16),
        scratch_shapes=[
            pltpu.VMEM((SB_SUB, 2 * BLOCK, BLOCK), _F32),
            pltpu.VMEM((SB_SUB, 2 * BLOCK, LANES), _F32),
        ],
        compiler_params=pltpu.CompilerParams(
            dimension_semantics=("arbitrary", "arbitrary", "arbitrary"),
            vmem_limit_bytes=VMEM_LIMIT),
        name="sb_attn",
    )(qkv, qkv, qkv, _suffix_sum_matrix())


def _sw_bias_kernel(bucket_ref, rel_ref, o_ref):
    bucket = bucket_ref[...]
    for h in range(SW_HEADS):
        acc = jnp.zeros(bucket.shape, _F32)
        for b in range(NUM_BUCKETS):
            acc = jnp.where(bucket == b, rel_ref[b, h], acc)
        o_ref[h] = acc


def _t5_bucket(dist):
    max_exact = NUM_BUCKETS // 2
    d = jnp.maximum(dist, 1).astype(_F32)
    large = max_exact + (jnp.log(d / max_exact) / math.log(MAX_DISTANCE / max_exact)
                         * (NUM_BUCKETS - max_exact)).astype(jnp.int32)
    large = jnp.minimum(large, NUM_BUCKETS - 1)
    return jnp.where(dist < max_exact, dist, large)


def _sw_bias(rel_bias):
    dist = (jnp.arange(BLOCK)[:, None] + BLOCK) - jnp.arange(2 * BLOCK)[None, :]
    bucket = _t5_bucket(jnp.maximum(dist, 0)).astype(jnp.int32)
    return pl.pallas_call(
        _sw_bias_kernel,
        in_specs=[
            pl.BlockSpec(memory_space=pltpu.VMEM),
            pl.BlockSpec(memory_space=pltpu.SMEM),
        ],
        out_specs=pl.BlockSpec(memory_space=pltpu.VMEM),
        out_shape=jax.ShapeDtypeStruct((SW_HEADS, BLOCK, 2 * BLOCK), _F32),
        name="sw_bias",
    )(bucket, rel_bias.astype(_F32))


SW_SUB = 4
SW_GROUP = SW_HEADS // SW_KV_HEADS


def _sw_kernel(q_ref, kp_ref, kc_ref, vp_ref, vc_ref, bias_ref, o_ref):
    rows = SW_GROUP * BLOCK
    half = rows // 2
    lane = lax.broadcasted_iota(jnp.int32, (rows, LANES), 1)
    odd_rows = lax.broadcasted_iota(jnp.int32, (rows, LANES), 0) >= half
    own_lanes = (lane >= HEAD_DIM) == odd_rows
    sink_key = lax.broadcasted_iota(jnp.int32, (2 * BLOCK, LANES), 0) == 0
    qpos = (lax.broadcasted_iota(jnp.int32, (rows, 2 * BLOCK), 0) & (BLOCK - 1)) + BLOCK
    kpos = lax.broadcasted_iota(jnp.int32, (rows, 2 * BLOCK), 1)
    dist = qpos - kpos
    in_window = (dist >= 0) & (dist < BLOCK)
    first_key = jnp.where(pl.program_id(1) > 0, 0, BLOCK)
    valid_first = (in_window & (kpos >= first_key)) | (kpos == 0)
    valid_later = in_window | (kpos == 0)
    head0_lanes = lax.broadcasted_iota(jnp.int32, (BLOCK, LANES), 1) < HEAD_DIM
    tiles_per_group = SW_GROUP * HEAD_DIM // LANES

    def kv_rows(prev_ref, cur_ref, blk, g):
        cols = slice(g * LANES, (g + 1) * LANES)
        cur = cur_ref[0, blk * BLOCK:(blk + 1) * BLOCK, cols]
        prev = (prev_ref[0, :, cols] if blk == 0
                else cur_ref[0, (blk - 1) * BLOCK:blk * BLOCK, cols])
        kv = jnp.concatenate([prev, cur], axis=0)
        return jnp.where(sink_key, jnp.zeros_like(kv), kv)

    def logits(blk, g):
        tiles = [q_ref[0, blk * BLOCK:(blk + 1) * BLOCK, c * LANES:(c + 1) * LANES]
                 for c in range(g * tiles_per_group, (g + 1) * tiles_per_group)]
        q4 = jnp.concatenate(tiles + tiles, axis=0)
        q4 = jnp.where(own_lanes, q4, jnp.zeros_like(q4))
        return _dot_nt(q4, kv_rows(kp_ref, kc_ref, blk, g))

    def exponentials(z, blk, g):
        z = jnp.where(valid_first if blk == 0 else valid_later, z + bias_ref[g], NEG_INF)
        return jnp.exp(z - jnp.max(z, axis=-1, keepdims=True)).astype(_BF16)

    def values(p, blk, g):
        vv = kv_rows(vp_ref, vc_ref, blk, g)
        pv = _dot(p, jnp.concatenate([vv, jnp.ones_like(vv)], axis=1))
        out = pv[:, :LANES] * (1.0 / pv[:, LANES:])
        for tile in range(tiles_per_group):
            c = g * tiles_per_group + tile
            o_ref[0, blk * BLOCK:(blk + 1) * BLOCK, c * LANES:(c + 1) * LANES] = jnp.where(
                head0_lanes, out[tile * BLOCK:(tile + 1) * BLOCK],
                out[half + tile * BLOCK:half + (tile + 1) * BLOCK]).astype(o_ref.dtype)

    items = [(blk, g) for blk in range(SW_SUB) for g in range(SW_KV_HEADS)]
    zs, ps = {}, {}
    for step in range(len(items) + 2):
        if step < len(items):
            zs[step] = logits(*items[step])
        if 0 <= step - 1 < len(items):
            ps[step - 1] = exponentials(zs.pop(step - 1), *items[step - 1])
        if 0 <= step - 2 < len(items):
            values(ps.pop(step - 2), *items[step - 2])


def _sw_attn(qkv, sinks, bias, *, q_col, k_col, v_col):
    b, s, _ = qkv.shape
    qw = SW_HEADS * HEAD_DIM
    kw = SW_KV_HEADS * LANES
    tq = SW_SUB * BLOCK
    assert s % tq == 0 and LANES == 2 * HEAD_DIM
    prev = lambda bi, n: (bi, jnp.maximum(n * SW_SUB - 1, 0))
    tiles = SW_GROUP * HEAD_DIM // LANES
    table = bias.at[:, :, 0].set(sinks.astype(_F32)[:, None])
    table = table.reshape(SW_KV_HEADS, tiles, 2, BLOCK, 2 * BLOCK).swapaxes(1, 2).reshape(
        SW_KV_HEADS, SW_GROUP * BLOCK, 2 * BLOCK)
    return pl.pallas_call(
        _sw_kernel,
        grid=(b, s // tq),
        in_specs=[
            pl.BlockSpec((1, tq, qw), lambda bi, n: (bi, n, q_col)),
            pl.BlockSpec((1, BLOCK, kw), lambda bi, n: prev(bi, n) + (k_col,)),
            pl.BlockSpec((1, tq, kw), lambda bi, n: (bi, n, k_col)),
            pl.BlockSpec((1, BLOCK, kw), lambda bi, n: prev(bi, n) + (v_col,)),
            pl.BlockSpec((1, tq, kw), lambda bi, n: (bi, n, v_col)),
            pl.BlockSpec((SW_KV_HEADS, SW_GROUP * BLOCK, 2 * BLOCK), lambda bi, n: (0, 0, 0)),
        ],
        out_specs=pl.BlockSpec((1, tq, qw), lambda bi, n: (bi, n, 0)),
        out_shape=jax.ShapeDtypeStruct((b, s, qw), _BF16),
        compiler_params=pltpu.CompilerParams(
            dimension_semantics=("arbitrary", "arbitrary"), vmem_limit_bytes=VMEM_LIMIT),
        name="sw_attn",
    )(qkv, qkv, qkv, qkv, qkv, table)


MIX_CHUNK = 256


def _mix_kernel(x_ref, g_ref, wg_ref, ysb_ref, ysw_ref, wsb_ref, wsw_ref, wo_ref, o_ref,
                h_scr, merged_scr):
    d = x_ref.shape[1]
    h_scr[...] = _rms(x_ref[...], g_ref[...]).astype(_BF16)

    def projections(c):
        cols = slice(c * MIX_CHUNK, (c + 1) * MIX_CHUNK)
        gate_cols = slice(d + c * MIX_CHUNK, d + (c + 1) * MIX_CHUNK)
        h = h_scr[...]
        return (_dot(h, wg_ref[:, cols]), _dot(ysb_ref[...], wsb_ref[:, cols]),
                _dot(h, wg_ref[:, gate_cols]), _dot(ysw_ref[...], wsw_ref[:, cols]))

    n_chunks = d // MIX_CHUNK
    ahead = projections(0)
    for c in range(n_chunks):
        logit_sb, y_sb, logit_sw, y_sw = ahead
        if c + 1 < n_chunks:
            ahead = projections(c + 1)
        merged = jax.nn.sigmoid(logit_sb) * y_sb + jax.nn.sigmoid(logit_sw) * y_sw
        merged_scr[:, c * MIX_CHUNK:(c + 1) * MIX_CHUNK] = merged.astype(_BF16)
    o_ref[...] = x_ref[...] + _dot(merged_scr[...], wo_ref[...])


def _mix_out(x2d, g, w_gate, y_sb, y_sw, w_sb, w_sw, w_out, *, tm):
    n, d = x2d.shape
    assert d % MIX_CHUNK == 0
    full = lambda a: pl.BlockSpec(a.shape, lambda i: (0, 0), pipeline_mode=pl.Buffered(1))
    rows = lambda a: pl.BlockSpec((tm, a.shape[1]), lambda i: (i, 0))
    return pl.pallas_call(
        _mix_kernel,
        grid=(n // tm,),
        in_specs=[rows(x2d), full(g), full(w_gate), rows(y_sb), rows(y_sw),
                  full(w_sb), full(w_sw), full(w_out)],
        out_specs=pl.BlockSpec((tm, d), lambda i: (i, 0)),
        out_shape=jax.ShapeDtypeStruct((n, d), _F32),
        scratch_shapes=[pltpu.VMEM((tm, d), _BF16), pltpu.VMEM((tm, d), _BF16)],
        compiler_params=pltpu.CompilerParams(
            dimension_semantics=("arbitrary",), vmem_limit_bytes=VMEM_LIMIT),
        name="mix_out",
    )(x2d, g, w_gate, y_sb, y_sw, w_sb, w_sw, w_out)


HALO = 8


def _ffn_kernel(x_ref, g_ref, wu_ref, cw_ref, cb_ref, wd_ref, gf_ref, o_ref,
                h_scr, act_scr, u_scr, carry_scr, *, tiles_per_seq):
    tm = x_ref.shape[0]
    d_ff = wd_ref.shape[0]
    nj = d_ff // FF_CHUNK
    h_scr[...] = _rms(x_ref[...], g_ref[...]).astype(_BF16)

    @pl.when(pl.program_id(0) % tiles_per_seq == 0)
    def _():
        carry_scr[...] = jnp.zeros_like(carry_scr)

    def up_proj(j):
        h = h_scr[...]
        gate = _dot(h, wu_ref[:, j * FF_CHUNK:(j + 1) * FF_CHUNK])
        value = _dot(h, wu_ref[:, d_ff + j * FF_CHUNK:d_ff + (j + 1) * FF_CHUNK])
        return gate, value

    def conv(j, part, u):
        slabs = []
        for k in range(FF_CHUNK // LANES):
            slab = ((j % 2) * 2 + part) * (FF_CHUNK // LANES) + k
            cols = slice(part * d_ff + j * FF_CHUNK + k * LANES,
                         part * d_ff + j * FF_CHUNK + (k + 1) * LANES)
            lanes = slice((part * FF_CHUNK // LANES + k) * LANES,
                          (part * FF_CHUNK // LANES + k + 1) * LANES)
            uk = u[:, k * LANES:(k + 1) * LANES]
            u_scr[slab, HALO:HALO + tm, :] = uk
            u_scr[slab, 0:HALO, :] = carry_scr[j, :, lanes]
            carry_scr[j, :, lanes] = uk[tm - HALO:tm, :]
            y = cb_ref[:, cols] + cw_ref[CONV_WIDTH - 1:CONV_WIDTH, cols] * uk
            for tap in range(CONV_WIDTH - 1):
                back = CONV_WIDTH - 1 - tap
                y = y + cw_ref[tap:tap + 1, cols] * u_scr[slab, pl.ds(HALO - back, tm, stride=1), :]
            slabs.append(y)
        return jnp.concatenate(slabs, axis=1)

    def gated_conv(j, gate, value):
        act = jax.nn.silu(conv(j, 0, gate)) * conv(j, 1, value)
        act_scr[:, j * FF_CHUNK:(j + 1) * FF_CHUNK] = act.astype(_BF16)

    ahead = up_proj(0)
    for j in range(nj):
        gate, value = ahead
        if j + 1 < nj:
            ahead = up_proj(j + 1)
        gated_conv(j, gate, value)
    o_ref[...] = _rms(x_ref[...] + _dot(act_scr[...], wd_ref[...]), gf_ref[...])


def _conv_ffn(x1, g_ffn, w_up, conv_w, conv_b, w_down, g_final, *, tm, tiles_per_seq):
    n, d = x1.shape
    d_ff = w_down.shape[0]
    nj = d_ff // FF_CHUNK
    resident = lambda a: pl.BlockSpec(a.shape, lambda i: (0, 0), pipeline_mode=pl.Buffered(1))
    return pl.pallas_call(
        functools.partial(_ffn_kernel, tiles_per_seq=tiles_per_seq),
        grid=(n // tm,),
        in_specs=[
            pl.BlockSpec((tm, d), lambda i: (i, 0)),
            resident(g_ffn), resident(w_up), resident(conv_w), resident(conv_b),
            resident(w_down), resident(g_final),
        ],
        out_specs=pl.BlockSpec((tm, d), lambda i: (i, 0)),
        out_shape=jax.ShapeDtypeStruct((n, d), _F32),
        scratch_shapes=[
            pltpu.VMEM((tm, d), _BF16),
            pltpu.VMEM((tm, d_ff), _BF16),
            pltpu.VMEM((2 * 2 * FF_CHUNK // LANES, HALO + tm, LANES), _F32),
            pltpu.VMEM((nj, HALO, 2 * FF_CHUNK), _F32),
        ],
        compiler_params=pltpu.CompilerParams(
            dimension_semantics=("arbitrary",), vmem_limit_bytes=VMEM_LIMIT),
        name="conv_ffn",
    )(x1, g_ffn, w_up, conv_w, conv_b, w_down, g_final)


def kernel(x, g_mix, w_in, w_sb_proj, w_sw_proj, w_out, rel_bias, sinks, g_ffn, w_up,
           conv_w, conv_b, w_down, g_final):
    b, s, d = x.shape
    n = b * s
    d_ff = w_down.shape[0]
    sb_w = SB_HEADS * HEAD_DIM
    sw_qw = SW_HEADS * HEAD_DIM
    sw_kw = SW_KV_HEADS * HEAD_DIM
    assert s % BLOCK == 0 and d_ff % FF_CHUNK == 0
    tm = min(512, s)
    assert s % tm == 0

    qkv_end = 3 * sb_w + sw_qw
    k_sw = w_in[:, qkv_end:qkv_end + sw_kw].reshape(d, SW_KV_HEADS, 1, HEAD_DIM)
    v_sw = w_in[:, qkv_end + sw_kw:qkv_end + 2 * sw_kw].reshape(d, SW_KV_HEADS, 1, HEAD_DIM)
    dup = lambda a: jnp.broadcast_to(
        a, (d, SW_KV_HEADS, LANES // HEAD_DIM, HEAD_DIM)).reshape(d, SW_KV_HEADS * LANES)
    w_qkv = jnp.concatenate([w_in[:, :qkv_end], dup(k_sw), dup(v_sw)], axis=1).astype(_BF16)
    w_gate = w_in[:, qkv_end + 2 * sw_kw:].astype(_BF16)
    row = lambda a: a.reshape(1, -1).astype(_F32)

    x2d = x.reshape(n, d)
    chunk = 512
    qkv = _in_proj(x2d, row(g_mix), w_qkv, tm=tm, chunk=chunk,
                   n_scaled=(0, 3 * sb_w // chunk))
    qkv = qkv.reshape(b, s, -1)
    y_sb = _sb_attn(qkv, q_col=0, k_col=sb_w // LANES, v_col=2 * sb_w // LANES)
    bias = _sw_bias(rel_bias)
    kv_tile = SW_KV_HEADS * LANES
    y_sw = _sw_attn(qkv, sinks, bias, q_col=3 * sb_w // sw_qw,
                    k_col=qkv_end // kv_tile, v_col=qkv_end // kv_tile + 1)
    x1 = _mix_out(x2d, row(g_mix), w_gate, y_sb.reshape(n, sb_w), y_sw.reshape(n, sw_qw),
                  w_sb_proj.astype(_BF16), w_sw_proj.astype(_BF16), w_out.astype(_BF16), tm=tm)
    out = _conv_ffn(x1, row(g_ffn), w_up.astype(_BF16), conv_w.astype(_F32), row(conv_b),
                    w_down.astype(_BF16), row(g_final), tm=tm, tiles_per_seq=s // tm)
    return out.reshape(b, s, d)
```

```python
import functools
import math

import jax
import jax.numpy as jnp
import numpy as np
from jax import lax
from jax.experimental import pallas as pl
from jax.experimental.pallas import tpu as pltpu

SB_HEADS = 8
SW_HEADS = 8
SW_KV_HEADS = 2
HEAD_DIM = 64
BLOCK = 128
NUM_BUCKETS = 32
MAX_DISTANCE = 128
CONV_WIDTH = 3
EPS = 1e-6
NEG_INF = -1e30
LANES = 128
FF_CHUNK = 256
VMEM_LIMIT = 56 * 1024 * 1024

_BF16 = jnp.bfloat16
_F32 = jnp.float32


def _rms(x, g):
    return x * lax.rsqrt(jnp.mean(x * x, axis=-1, keepdims=True) + EPS) * g


def _dot(a, b):
    return jnp.dot(a, b, preferred_element_type=_F32)


def _dot_nt(a, b):
    return lax.dot_general(a, b, (((1,), (1,)), ((), ())), preferred_element_type=_F32)


ROW_SUB = 2


def _in_proj_kernel(x_ref, g_ref, w_ref, o_ref, h_scr, *, n_scaled, chunk):
    tm = x_ref.shape[0] // ROW_SUB
    rows = lambda s: slice(s * tm, (s + 1) * tm)
    scale = 1.0 / math.sqrt(HEAD_DIM)

    def prologue(s):
        h_scr[rows(s), :] = _rms(x_ref[rows(s), :], g_ref[...]).astype(_BF16)

    prologue(0)
    for s in range(ROW_SUB):
        for c in range(w_ref.shape[1] // chunk):
            cols = slice(c * chunk, (c + 1) * chunk)
            p = _dot(h_scr[rows(s), :], w_ref[:, cols])
            if c in n_scaled:
                p = p * scale
            o_ref[rows(s), cols] = p.astype(_BF16)
            if c == 0 and s + 1 < ROW_SUB:
                prologue(s + 1)


def _in_proj(x2d, g, w, *, tm, n_scaled, chunk):
    n, d = x2d.shape
    width = w.shape[1]
    step_rows = ROW_SUB * tm
    return pl.pallas_call(
        functools.partial(_in_proj_kernel, n_scaled=n_scaled, chunk=chunk),
        grid=(n // step_rows,),
        in_specs=[
            pl.BlockSpec((step_rows, d), lambda i: (i, 0)),
            pl.BlockSpec((1, d), lambda i: (0, 0), pipeline_mode=pl.Buffered(1)),
            pl.BlockSpec((d, width), lambda i: (0, 0), pipeline_mode=pl.Buffered(1)),
        ],
        out_specs=pl.BlockSpec((step_rows, width), lambda i: (i, 0)),
        out_shape=jax.ShapeDtypeStruct((n, width), _BF16),
        scratch_shapes=[pltpu.VMEM((step_rows, d), _BF16)],
        compiler_params=pltpu.CompilerParams(
            dimension_semantics=("arbitrary",), vmem_limit_bytes=VMEM_LIMIT),
        name="in_proj",
    )(x2d, g, w)


SB_SUB = 16
SB_TOP = 32
SB_MASKED = -1e30
SB_SKEW = 3
SB_OFF = 1e30
LOG2E = 1.4426950408889634
SB_DEAD_LOG2 = 151.0


def _sb_kernel(q_ref, k_ref, v_ref, t_ref, o_ref, rem_scr, acc_scr):
    base = pl.program_id(2) * SB_SUB
    tri = t_ref[...]
    rows2, top2 = 2 * BLOCK, 2 * SB_TOP
    rest = BLOCK - SB_TOP
    lane = lax.broadcasted_iota(jnp.int32, (rows2, LANES), 1)
    row = lax.broadcasted_iota(jnp.int32, (rows2, LANES), 0)
    head1 = ((row >= SB_TOP) & (row < top2)) | (row >= top2 + rest)
    own_lanes = (lane >= HEAD_DIM) == head1
    query = jnp.where(row < SB_TOP, row, jnp.where(row < top2 + rest, row - SB_TOP, row - BLOCK))
    strictly_earlier = lane < query

    def stacked_q(a):
        q = q_ref[0, a * BLOCK:(a + 1) * BLOCK, :]
        q2 = jnp.concatenate([q[:SB_TOP], q[:SB_TOP], q[SB_TOP:], q[SB_TOP:]], axis=0)
        return jnp.where(own_lanes, q2, jnp.zeros_like(q2))

    def key_rows(kb):
        return pl.ds(pl.multiple_of(kb * BLOCK, BLOCK), BLOCK)

    def scores(q2, kb, diagonal):
        z = _dot_nt(q2, k_ref[0, key_rows(kb), :]) * LOG2E
        return jnp.where(strictly_earlier, z, SB_MASKED) if diagonal else z

    def suffix_mass(z):
        sign = jnp.uint32(0x80000000)
        neg_abs = lax.bitcast_convert_type(lax.bitcast_convert_type(z, jnp.uint32) | sign, _F32)
        sp = jnp.maximum(z, 0.0) + jnp.log2(1.0 + jnp.exp2(neg_abs))
        hi = sp.astype(_BF16)
        lo = (sp - hi.astype(_F32)).astype(_BF16)
        return _dot(jnp.concatenate([hi, lo], axis=1), tri)

    def weighted_values(z, cum, rem, kb):
        w = jnp.exp2(z - cum[:, :BLOCK] - rem)
        return _dot(w.astype(_BF16), v_ref[0, key_rows(kb), :])

    def earlier_block(a, t, rem):
        kb = base + a - t
        return jnp.maximum(kb, 0), rem + jnp.where(kb < 0, SB_OFF, 0.0)

    items = [(a, t, slice(0, top2) if t == 2 else slice(0, rows2))
             for a in range(SB_SUB) for t in range(3)]
    q2s, zs, cums = {}, {}, {}
    rems = [jnp.zeros((rows2, BLOCK), _F32) for _ in range(SB_SUB)]
    accs = [jnp.zeros((rows2, LANES), _F32) for _ in range(SB_SUB)]
    for step in range(len(items) + 2 * SB_SKEW):
        if step < len(items):
            a, t, part = items[step]
            if t == 0:
                q2s[a] = stacked_q(a)
            zs[step] = scores(q2s[a][part], jnp.maximum(base + a - t, 0), t == 0)
        i = step - SB_SKEW
        if 0 <= i < len(items):
            cums[i] = suffix_mass(zs[i])
        i = step - 2 * SB_SKEW
        if 0 <= i < len(items):
            a, t, part = items[i]
            kb, rem = earlier_block(a, t, rems[a][part])
            pv = weighted_values(zs.pop(i), cums[i], rem, kb)
            rem = rem + cums.pop(i)[:, BLOCK:]
            acc = accs[a][part] + pv
            if part.stop < rows2:
                rem = jnp.concatenate([rem, rems[a][part.stop:]], axis=0)
                acc = jnp.concatenate([acc, accs[a][part.stop:]], axis=0)
            rems[a], accs[a] = rem, acc
    floor = None
    for a in range(SB_SUB):
        rem_scr[a] = rems[a]
        acc_scr[a] = accs[a]
        floor = rems[a] if floor is None else jnp.minimum(floor, rems[a])

    def cond(carry):
        t, least = carry
        return (base + SB_SUB - 1 - t >= 0) & (least < SB_DEAD_LOG2)

    def body(carry):
        t, _ = carry
        floor = None
        for a in range(SB_SUB):
            q2 = stacked_q(a)
            for part, back in ((slice(top2, rows2), t), (slice(0, top2), t + 1)):
                kb, rem = earlier_block(a, back, rem_scr[a, part, :])
                z = scores(q2[part], kb, False)
                cum = suffix_mass(z)
                acc_scr[a, part, :] += weighted_values(z, cum, rem, kb)
                rem = rem + cum[:, BLOCK:]
                rem_scr[a, part, :] = rem
                least = jnp.min(rem, axis=0, keepdims=True)
                floor = least if floor is None else jnp.minimum(floor, least)
        return t + 1, jnp.min(floor)

    lax.while_loop(cond, body, (jnp.int32(2), jnp.min(floor)))
    head0_lanes = lax.broadcasted_iota(jnp.int32, (BLOCK, LANES), 1) < HEAD_DIM
    for a in range(SB_SUB):
        head0 = jnp.concatenate([acc_scr[a, :SB_TOP, :], acc_scr[a, top2:top2 + rest, :]], axis=0)
        head1 = jnp.concatenate([acc_scr[a, SB_TOP:top2, :], acc_scr[a, top2 + rest:, :]], axis=0)
        o_ref[0, a * BLOCK:(a + 1) * BLOCK, :] = jnp.where(
            head0_lanes, head0, head1).astype(o_ref.dtype)


def _suffix_sum_matrix():
    j = np.arange(2 * BLOCK)[:, None] % BLOCK
    s = np.arange(2 * BLOCK)[None, :]
    return jnp.asarray(((s >= BLOCK) | (j >= s)).astype(np.float32), dtype=_BF16)


def _sb_attn(qkv, *, q_col, k_col, v_col):
    b, s, _ = qkv.shape
    pairs = SB_HEADS * HEAD_DIM // LANES
    tq = SB_SUB * BLOCK
    assert s % tq == 0
    return pl.pallas_call(
        _sb_kernel,
        grid=(b, pairs, s // tq),
        in_specs=[
            pl.BlockSpec((1, tq, LANES), lambda bi, p, qi: (bi, qi, q_col + p)),
            pl.BlockSpec((1, s, LANES), lambda bi, p, qi: (bi, 0, k_col + p)),
            pl.BlockSpec((1, s, LANES), lambda bi, p, qi: (bi, 0, v_col + p)),
            pl.BlockSpec((2 * BLOCK, 2 * BLOCK), lambda bi, p, qi: (0, 0)),
        ],
        out_specs=pl.BlockSpec((1, tq, LANES), lambda bi, p, qi: (bi, qi, p)),
        out_shape=jax.ShapeDtypeStruct((b, s, SB_HEADS * HEAD_DIM), _BF16),
        scratch_shapes=[
            pltpu.VMEM((SB_SUB, 2 * BLOCK, BLOCK), _F32),
            pltpu.VMEM((SB_SUB, 2 * BLOCK, LANES), _F32),
        ],
        compiler_params=pltpu.CompilerParams(
            dimension_semantics=("arbitrary", "arbitrary", "arbitrary"),
            vmem_limit_bytes=VMEM_LIMIT),
        name="sb_attn",
    )(qkv, qkv, qkv, _suffix_sum_matrix())


def _sw_bias_kernel(bucket_ref, rel_ref, o_ref):
    bucket = bucket_ref[...]
    for h in range(SW_HEADS):
        acc = jnp.zeros(bucket.shape, _F32)
        for b in range(NUM_BUCKETS):
            acc = jnp.where(bucket == b, rel_ref[b, h], acc)
        o_ref[h] = acc


def _t5_bucket(dist):
    max_exact = NUM_BUCKETS // 2
    d = jnp.maximum(dist, 1).astype(_F32)
    large = max_exact + (jnp.log(d / max_exact) / math.log(MAX_DISTANCE / max_exact)
                         * (NUM_BUCKETS - max_exact)).astype(jnp.int32)
    large = jnp.minimum(large, NUM_BUCKETS - 1)
    return jnp.where(dist < max_exact, dist, large)


def _sw_bias(rel_bias):
    dist = (jnp.arange(BLOCK)[:, None] + BLOCK) - jnp.arange(2 * BLOCK)[None, :]
    bucket = _t5_bucket(jnp.maximum(dist, 0)).astype(jnp.int32)
    return pl.pallas_call(
        _sw_bias_kernel,
        in_specs=[
            pl.BlockSpec(memory_space=pltpu.VMEM),
            pl.BlockSpec(memory_space=pltpu.SMEM),
        ],
        out_specs=pl.BlockSpec(memory_space=pltpu.VMEM),
        out_shape=jax.ShapeDtypeStruct((SW_HEADS, BLOCK, 2 * BLOCK), _F32),
        name="sw_bias",
    )(bucket, rel_bias.astype(_F32))


SW_SUB = 4
SW_GROUP = SW_HEADS // SW_KV_HEADS


def _sw_kernel(q_ref, kp_ref, kc_ref, vp_ref, vc_ref, bias_ref, o_ref):
    rows = SW_GROUP * BLOCK
    half = rows // 2
    lane = lax.broadcasted_iota(jnp.int32, (rows, LANES), 1)
    odd_rows = lax.broadcasted_iota(jnp.int32, (rows, LANES), 0) >= half
    own_lanes = (lane >= HEAD_DIM) == odd_rows
    sink_key = lax.broadcasted_iota(jnp.int32, (2 * BLOCK, LANES), 0) == 0
    qpos = (lax.broadcasted_iota(jnp.int32, (rows, 2 * BLOCK), 0) & (BLOCK - 1)) + BLOCK
    kpos = lax.broadcasted_iota(jnp.int32, (rows, 2 * BLOCK), 1)
    dist = qpos - kpos
    in_window = (dist >= 0) & (dist < BLOCK)
    first_key = jnp.where(pl.program_id(1) > 0, 0, BLOCK)
    valid_first = (in_window & (kpos >= first_key)) | (kpos == 0)
    valid_later = in_window | (kpos == 0)
    head0_lanes = lax.broadcasted_iota(jnp.int32, (BLOCK, LANES), 1) < HEAD_DIM
    tiles_per_group = SW_GROUP * HEAD_DIM // LANES

    def kv_rows(prev_ref, cur_ref, blk, g):
        cols = slice(g * LANES, (g + 1) * LANES)
        cur = cur_ref[0, blk * BLOCK:(blk + 1) * BLOCK, cols]
        prev = (prev_ref[0, :, cols] if blk == 0
                else cur_ref[0, (blk - 1) * BLOCK:blk * BLOCK, cols])
        kv = jnp.concatenate([prev, cur], axis=0)
        return jnp.where(sink_key, jnp.zeros_like(kv), kv)

    def logits(blk, g):
        tiles = [q_ref[0, blk * BLOCK:(blk + 1) * BLOCK, c * LANES:(c + 1) * LANES]
                 for c in range(g * tiles_per_group, (g + 1) * tiles_per_group)]
        q4 = jnp.concatenate(tiles + tiles, axis=0)
        q4 = jnp.where(own_lanes, q4, jnp.zeros_like(q4))
        return _dot_nt(q4, kv_rows(kp_ref, kc_ref, blk, g))

    def exponentials(z, blk, g):
        z = jnp.where(valid_first if blk == 0 else valid_later, z + bias_ref[g], NEG_INF)
        return jnp.exp(z - jnp.max(z, axis=-1, keepdims=True)).astype(_BF16)

    def values(p, blk, g):
        vv = kv_rows(vp_ref, vc_ref, blk, g)
        pv = _dot(p, jnp.concatenate([vv, jnp.ones_like(vv)], axis=1))
        out = pv[:, :LANES] * (1.0 / pv[:, LANES:])
        for tile in range(tiles_per_group):
            c = g * tiles_per_group + tile
            o_ref[0, blk * BLOCK:(blk + 1) * BLOCK, c * LANES:(c + 1) * LANES] = jnp.where(
                head0_lanes, out[tile * BLOCK:(tile + 1) * BLOCK],
                out[half + tile * BLOCK:half + (tile + 1) * BLOCK]).astype(o_ref.dtype)

    items = [(blk, g) for blk in range(SW_SUB) for g in range(SW_KV_HEADS)]
    zs, ps = {}, {}
    for step in range(len(items) + 2):
        if step < len(items):
            zs[step] = logits(*items[step])
        if 0 <= step - 1 < len(items):
            ps[step - 1] = exponentials(zs.pop(step - 1), *items[step - 1])
        if 0 <= step - 2 < len(items):
            values(ps.pop(step - 2), *items[step - 2])


def _sw_attn(qkv, sinks, bias, *, q_col, k_col, v_col):
    b, s, _ = qkv.shape
    qw = SW_HEADS * HEAD_DIM
    kw = SW_KV_HEADS * LANES
    tq = SW_SUB * BLOCK
    assert s % tq == 0 and LANES == 2 * HEAD_DIM
    prev = lambda bi, n: (bi, jnp.maximum(n * SW_SUB - 1, 0))
    tiles = SW_GROUP * HEAD_DIM // LANES
    table = bias.at[:, :, 0].set(sinks.astype(_F32)[:, None])
    table = table.reshape(SW_KV_HEADS, tiles, 2, BLOCK, 2 * BLOCK).swapaxes(1, 2).reshape(
        SW_KV_HEADS, SW_GROUP * BLOCK, 2 * BLOCK)
    return pl.pallas_call(
        _sw_kernel,
        grid=(b, s // tq),
        in_specs=[
            pl.BlockSpec((1, tq, qw), lambda bi, n: (bi, n, q_col)),
            pl.BlockSpec((1, BLOCK, kw), lambda bi, n: prev(bi, n) + (k_col,)),
            pl.BlockSpec((1, tq, kw), lambda bi, n: (bi, n, k_col)),
            pl.BlockSpec((1, BLOCK, kw), lambda bi, n: prev(bi, n) + (v_col,)),
            pl.BlockSpec((1, tq, kw), lambda bi, n: (bi, n, v_col)),
            pl.BlockSpec((SW_KV_HEADS, SW_GROUP * BLOCK, 2 * BLOCK), lambda bi, n: (0, 0, 0)),
        ],
        out_specs=pl.BlockSpec((1, tq, qw), lambda bi, n: (bi, n, 0)),
        out_shape=jax.ShapeDtypeStruct((b, s, qw), _BF16),
        compiler_params=pltpu.CompilerParams(
            dimension_semantics=("arbitrary", "arbitrary"), vmem_limit_bytes=VMEM_LIMIT),
        name="sw_attn",
    )(qkv, qkv, qkv, qkv, qkv, table)


MIX_CHUNK = 256


def _mix_kernel(x_ref, g_ref, wg_ref, ysb_ref, ysw_ref, wsb_ref, wsw_ref, wo_ref, o_ref,
                h_scr, merged_scr):
    d = x_ref.shape[1]
    tm = x_ref.shape[0] // ROW_SUB
    n_chunks = d // MIX_CHUNK
    rows = lambda s: slice(s * tm, (s + 1) * tm)

    def prologue(s):
        h_scr[rows(s), :] = _rms(x_ref[rows(s), :], g_ref[...]).astype(_BF16)

    def projections(s, c):
        cols = slice(c * MIX_CHUNK, (c + 1) * MIX_CHUNK)
        gate_cols = slice(d + c * MIX_CHUNK, d + (c + 1) * MIX_CHUNK)
        h = h_scr[rows(s), :]
        return (_dot(h, wg_ref[:, cols]), _dot(ysb_ref[rows(s), :], wsb_ref[:, cols]),
                _dot(h, wg_ref[:, gate_cols]), _dot(ysw_ref[rows(s), :], wsw_ref[:, cols]))

    def epilogue(s, y):
        o_ref[rows(s), :] = x_ref[rows(s), :] + y

    prologue(0)
    ahead = projections(0, 0)
    pending = None
    for s in range(ROW_SUB):
        for c in range(n_chunks):
            logit_sb, y_sb, logit_sw, y_sw = ahead
            if c + 1 < n_chunks:
                ahead = projections(s, c + 1)
            elif s + 1 < ROW_SUB:
                ahead = projections(s + 1, 0)
            merged = jax.nn.sigmoid(logit_sb) * y_sb + jax.nn.sigmoid(logit_sw) * y_sw
            merged_scr[rows(s), c * MIX_CHUNK:(c + 1) * MIX_CHUNK] = merged.astype(_BF16)
            if c == 0 and s + 1 < ROW_SUB:
                prologue(s + 1)
            if c == 1 and pending is not None:
                epilogue(*pending)
                pending = None
        pending = (s, _dot(merged_scr[rows(s), :], wo_ref[...]))
    epilogue(*pending)


def _mix_out(x2d, g, w_gate, y_sb, y_sw, w_sb, w_sw, w_out, *, tm):
    n, d = x2d.shape
    assert d % MIX_CHUNK == 0
    step_rows = ROW_SUB * tm
    full = lambda a: pl.BlockSpec(a.shape, lambda i: (0, 0), pipeline_mode=pl.Buffered(1))
    rows = lambda a: pl.BlockSpec((step_rows, a.shape[1]), lambda i: (i, 0))
    return pl.pallas_call(
        _mix_kernel,
        grid=(n // step_rows,),
        in_specs=[rows(x2d), full(g), full(w_gate), rows(y_sb), rows(y_sw),
                  full(w_sb), full(w_sw), full(w_out)],
        out_specs=pl.BlockSpec((step_rows, d), lambda i: (i, 0)),
        out_shape=jax.ShapeDtypeStruct((n, d), _F32),
        scratch_shapes=[pltpu.VMEM((step_rows, d), _BF16), pltpu.VMEM((step_rows, d), _BF16)],
        compiler_params=pltpu.CompilerParams(
            dimension_semantics=("arbitrary",), vmem_limit_bytes=VMEM_LIMIT),
        name="mix_out",
    )(x2d, g, w_gate, y_sb, y_sw, w_sb, w_sw, w_out)


FFN_SUB = 2
HALO = 8


def _ffn_kernel(x_ref, g_ref, wu_ref, cw_ref, cb_ref, wd_ref, gf_ref, o_ref,
                h_scr, act_scr, u_scr, carry_scr, *, steps_per_seq):
    tm = x_ref.shape[0] // FFN_SUB
    d_ff = wd_ref.shape[0]
    nj = d_ff // FF_CHUNK
    rows = lambda s: slice(s * tm, (s + 1) * tm)

    @pl.when(pl.program_id(0) % steps_per_seq == 0)
    def _():
        carry_scr[...] = jnp.zeros_like(carry_scr)

    def prologue(s):
        h_scr[rows(s), :] = _rms(x_ref[rows(s), :], g_ref[...]).astype(_BF16)

    def up_proj(s, j):
        h = h_scr[rows(s), :]
        gate = _dot(h, wu_ref[:, j * FF_CHUNK:(j + 1) * FF_CHUNK])
        value = _dot(h, wu_ref[:, d_ff + j * FF_CHUNK:d_ff + (j + 1) * FF_CHUNK])
        return gate, value

    def conv(slot, j, part, u):
        slabs = []
        for k in range(FF_CHUNK // LANES):
            slab = (slot * 2 + part) * (FF_CHUNK // LANES) + k
            cols = slice(part * d_ff + j * FF_CHUNK + k * LANES,
                         part * d_ff + j * FF_CHUNK + (k + 1) * LANES)
            lanes = slice((part * FF_CHUNK // LANES + k) * LANES,
                          (part * FF_CHUNK // LANES + k + 1) * LANES)
            uk = u[:, k * LANES:(k + 1) * LANES]
            u_scr[slab, HALO:HALO + tm, :] = uk
            u_scr[slab, 0:HALO, :] = carry_scr[j, :, lanes]
            carry_scr[j, :, lanes] = uk[tm - HALO:tm, :]
            y = cb_ref[:, cols] + cw_ref[CONV_WIDTH - 1:CONV_WIDTH, cols] * uk
            for tap in range(CONV_WIDTH - 1):
                back = CONV_WIDTH - 1 - tap
                y = y + cw_ref[tap:tap + 1, cols] * u_scr[slab, pl.ds(HALO - back, tm, stride=1), :]
            slabs.append(y)
        return jnp.concatenate(slabs, axis=1)

    def gated_conv(s, j, gate, value):
        slot = (s * nj + j) % 2
        act = jax.nn.silu(conv(slot, j, 0, gate)) * conv(slot, j, 1, value)
        act_scr[rows(s), j * FF_CHUNK:(j + 1) * FF_CHUNK] = act.astype(_BF16)

    def epilogue(s, y):
        o_ref[rows(s), :] = _rms(x_ref[rows(s), :] + y, gf_ref[...])

    prologue(0)
    ahead = up_proj(0, 0)
    pending = None
    for s in range(FFN_SUB):
        for j in range(nj):
            gate, value = ahead
            if j + 1 < nj:
                ahead = up_proj(s, j + 1)
            elif s + 1 < FFN_SUB:
                ahead = up_proj(s + 1, 0)
            gated_conv(s, j, gate, value)
            if j == 0 and s + 1 < FFN_SUB:
                prologue(s + 1)
            if j == 1 and pending is not None:
                epilogue(*pending)
                pending = None
        pending = (s, _dot(act_scr[rows(s), :], wd_ref[...]))
    epilogue(*pending)


def _conv_ffn(x1, g_ffn, w_up, conv_w, conv_b, w_down, g_final, *, tm, tiles_per_seq):
    n, d = x1.shape
    d_ff = w_down.shape[0]
    nj = d_ff // FF_CHUNK
    assert tiles_per_seq % FFN_SUB == 0
    step_rows = FFN_SUB * tm
    resident = lambda a: pl.BlockSpec(a.shape, lambda i: (0, 0), pipeline_mode=pl.Buffered(1))
    return pl.pallas_call(
        functools.partial(_ffn_kernel, steps_per_seq=tiles_per_seq // FFN_SUB),
        grid=(n // step_rows,),
        in_specs=[
            pl.BlockSpec((step_rows, d), lambda i: (i, 0)),
            resident(g_ffn), resident(w_up), resident(conv_w), resident(conv_b),
            resident(w_down), resident(g_final),
        ],
        out_specs=pl.BlockSpec((step_rows, d), lambda i: (i, 0)),
        out_shape=jax.ShapeDtypeStruct((n, d), _F32),
        scratch_shapes=[
            pltpu.VMEM((step_rows, d), _BF16),
            pltpu.VMEM((step_rows, d_ff), _BF16),
            pltpu.VMEM((2 * 2 * FF_CHUNK // LANES, HALO + tm, LANES), _F32),
            pltpu.VMEM((nj, HALO, 2 * FF_CHUNK), _F32),
        ],
        compiler_params=pltpu.CompilerParams(
            dimension_semantics=("arbitrary",), vmem_limit_bytes=VMEM_LIMIT),
        name="conv_ffn",
    )(x1, g_ffn, w_up, conv_w, conv_b, w_down, g_final)


def kernel(x, g_mix, w_in, w_sb_proj, w_sw_proj, w_out, rel_bias, sinks, g_ffn, w_up,
           conv_w, conv_b, w_down, g_final):
    b, s, d = x.shape
    n = b * s
    d_ff = w_down.shape[0]
    sb_w = SB_HEADS * HEAD_DIM
    sw_qw = SW_HEADS * HEAD_DIM
    sw_kw = SW_KV_HEADS * HEAD_DIM
    assert s % BLOCK == 0 and d_ff % FF_CHUNK == 0
    tm = min(512, s)
    assert s % (max(ROW_SUB, FFN_SUB) * tm) == 0

    qkv_end = 3 * sb_w + sw_qw
    k_sw = w_in[:, qkv_end:qkv_end + sw_kw].reshape(d, SW_KV_HEADS, 1, HEAD_DIM)
    v_sw = w_in[:, qkv_end + sw_kw:qkv_end + 2 * sw_kw].reshape(d, SW_KV_HEADS, 1, HEAD_DIM)
    dup = lambda a: jnp.broadcast_to(
        a, (d, SW_KV_HEADS, LANES // HEAD_DIM, HEAD_DIM)).reshape(d, SW_KV_HEADS * LANES)
    w_qkv = jnp.concatenate([w_in[:, :qkv_end], dup(k_sw), dup(v_sw)], axis=1).astype(_BF16)
    w_gate = w_in[:, qkv_end + 2 * sw_kw:].astype(_BF16)
    row = lambda a: a.reshape(1, -1).astype(_F32)

    x2d = x.reshape(n, d)
    chunk = 512
    qkv = _in_proj(x2d, row(g_mix), w_qkv, tm=tm, chunk=chunk,
                   n_scaled=(0, 3 * sb_w // chunk))
    qkv = qkv.reshape(b, s, -1)
    y_sb = _sb_attn(qkv, q_col=0, k_col=sb_w // LANES, v_col=2 * sb_w // LANES)
    bias = _sw_bias(rel_bias)
    kv_tile = SW_KV_HEADS * LANES
    y_sw = _sw_attn(qkv, sinks, bias, q_col=3 * sb_w // sw_qw,
                    k_col=qkv_end // kv_tile, v_col=qkv_end // kv_tile + 1)
    x1 = _mix_out(x2d, row(g_mix), w_gate, y_sb.reshape(n, sb_w), y_sw.reshape(n, sw_qw),
                  w_sb_proj.astype(_BF16), w_sw_proj.astype(_BF16), w_out.astype(_BF16), tm=tm)
    out = _conv_ffn(x1, row(g_ffn), w_up.astype(_BF16), conv_w.astype(_F32), row(conv_b),
                    w_down.astype(_BF16), row(g_final), tm=tm, tiles_per_seq=s // tm)
    return out.reshape(b, s, d)
```

```python
import functools
import math

import jax
import jax.numpy as jnp
import numpy as np
from jax import lax
from jax.experimental import pallas as pl
from jax.experimental.pallas import tpu as pltpu

SB_HEADS = 8
SW_HEADS = 8
SW_KV_HEADS = 2
HEAD_DIM = 64
BLOCK = 128
NUM_BUCKETS = 32
MAX_DISTANCE = 128
CONV_WIDTH = 3
EPS = 1e-6
NEG_INF = -1e30
LANES = 128
FF_CHUNK = 256
VMEM_LIMIT = 56 * 1024 * 1024

_BF16 = jnp.bfloat16
_F32 = jnp.float32


def _rms(x, g):
    return x * lax.rsqrt(jnp.mean(x * x, axis=-1, keepdims=True) + EPS) * g


def _dot(a, b):
    return jnp.dot(a, b, preferred_element_type=_F32)


def _dot_nt(a, b):
    return lax.dot_general(a, b, (((1,), (1,)), ((), ())), preferred_element_type=_F32)


ROW_SUB = 2


def _in_proj_kernel(x_ref, g_ref, w_ref, o_ref, h_scr, *, n_scaled, chunk):
    tm = x_ref.shape[0] // ROW_SUB
    rows = lambda s: slice(s * tm, (s + 1) * tm)
    scale = 1.0 / math.sqrt(HEAD_DIM)

    def prologue(s):
        h_scr[rows(s), :] = _rms(x_ref[rows(s), :], g_ref[...]).astype(_BF16)

    prologue(0)
    width = w_ref.shape[1]
    for s in range(ROW_SUB):
        for c, start in enumerate(range(0, width, chunk)):
            cols = slice(start, min(start + chunk, width))
            p = _dot(h_scr[rows(s), :], w_ref[:, cols])
            if c in n_scaled:
                p = p * scale
            o_ref[rows(s), cols] = p.astype(_BF16)
            if c == 0 and s + 1 < ROW_SUB:
                prologue(s + 1)


def _in_proj(x2d, g, w, *, tm, n_scaled, chunk):
    n, d = x2d.shape
    width = w.shape[1]
    step_rows = ROW_SUB * tm
    return pl.pallas_call(
        functools.partial(_in_proj_kernel, n_scaled=n_scaled, chunk=chunk),
        grid=(n // step_rows,),
        in_specs=[
            pl.BlockSpec((step_rows, d), lambda i: (i, 0)),
            pl.BlockSpec((1, d), lambda i: (0, 0), pipeline_mode=pl.Buffered(1)),
            pl.BlockSpec((d, width), lambda i: (0, 0), pipeline_mode=pl.Buffered(1)),
        ],
        out_specs=pl.BlockSpec((step_rows, width), lambda i: (i, 0)),
        out_shape=jax.ShapeDtypeStruct((n, width), _BF16),
        scratch_shapes=[pltpu.VMEM((step_rows, d), _BF16)],
        compiler_params=pltpu.CompilerParams(
            dimension_semantics=("arbitrary",), vmem_limit_bytes=VMEM_LIMIT),
        name="in_proj",
    )(x2d, g, w)


SB_SUB = 16
SB_TOP = 32
SB_MASKED = -1e30
SB_SKEW = 3
SB_OFF = 1e30
LOG2E = 1.4426950408889634
SB_DEAD_LOG2 = 151.0


def _sb_kernel(q_ref, k_ref, v_ref, t_ref, o_ref, rem_scr, acc_scr):
    base = pl.program_id(2) * SB_SUB
    tri = t_ref[...]
    rows2, top2 = 2 * BLOCK, 2 * SB_TOP
    rest = BLOCK - SB_TOP
    lane = lax.broadcasted_iota(jnp.int32, (rows2, LANES), 1)
    row = lax.broadcasted_iota(jnp.int32, (rows2, LANES), 0)
    head1 = ((row >= SB_TOP) & (row < top2)) | (row >= top2 + rest)
    own_lanes = (lane >= HEAD_DIM) == head1
    query = jnp.where(row < SB_TOP, row, jnp.where(row < top2 + rest, row - SB_TOP, row - BLOCK))
    strictly_earlier = lane < query

    def stacked_q(a):
        q = q_ref[0, a * BLOCK:(a + 1) * BLOCK, :]
        q2 = jnp.concatenate([q[:SB_TOP], q[:SB_TOP], q[SB_TOP:], q[SB_TOP:]], axis=0)
        return jnp.where(own_lanes, q2, jnp.zeros_like(q2))

    def key_rows(kb):
        return pl.ds(pl.multiple_of(kb * BLOCK, BLOCK), BLOCK)

    def scores(q2, kb, diagonal):
        z = _dot_nt(q2, k_ref[0, key_rows(kb), :]) * LOG2E
        return jnp.where(strictly_earlier, z, SB_MASKED) if diagonal else z

    def suffix_mass(z):
        sign = jnp.uint32(0x80000000)
        neg_abs = lax.bitcast_convert_type(lax.bitcast_convert_type(z, jnp.uint32) | sign, _F32)
        sp = jnp.maximum(z, 0.0) + jnp.log2(1.0 + jnp.exp2(neg_abs))
        hi = sp.astype(_BF16)
        lo = (sp - hi.astype(_F32)).astype(_BF16)
        return _dot(jnp.concatenate([hi, lo], axis=1), tri)

    def weighted_values(z, cum, rem, kb):
        w = jnp.exp2(z - cum[:, :BLOCK] - rem)
        return _dot(w.astype(_BF16), v_ref[0, key_rows(kb), :])

    def earlier_block(a, t, rem):
        kb = base + a - t
        return jnp.maximum(kb, 0), rem + jnp.where(kb < 0, SB_OFF, 0.0)

    items = [(a, t, slice(0, top2) if t == 2 else slice(0, rows2))
             for a in range(SB_SUB) for t in range(3)]
    q2s, zs, cums = {}, {}, {}
    rems = [jnp.zeros((rows2, BLOCK), _F32) for _ in range(SB_SUB)]
    accs = [jnp.zeros((rows2, LANES), _F32) for _ in range(SB_SUB)]
    for step in range(len(items) + 2 * SB_SKEW):
        if step < len(items):
            a, t, part = items[step]
            if t == 0:
                q2s[a] = stacked_q(a)
            zs[step] = scores(q2s[a][part], jnp.maximum(base + a - t, 0), t == 0)
        i = step - SB_SKEW
        if 0 <= i < len(items):
            cums[i] = suffix_mass(zs[i])
        i = step - 2 * SB_SKEW
        if 0 <= i < len(items):
            a, t, part = items[i]
            kb, rem = earlier_block(a, t, rems[a][part])
            pv = weighted_values(zs.pop(i), cums[i], rem, kb)
            rem = rem + cums.pop(i)[:, BLOCK:]
            acc = accs[a][part] + pv
            if part.stop < rows2:
                rem = jnp.concatenate([rem, rems[a][part.stop:]], axis=0)
                acc = jnp.concatenate([acc, accs[a][part.stop:]], axis=0)
            rems[a], accs[a] = rem, acc
    floor = None
    for a in range(SB_SUB):
        rem_scr[a] = rems[a]
        acc_scr[a] = accs[a]
        floor = rems[a] if floor is None else jnp.minimum(floor, rems[a])

    def cond(carry):
        t, least = carry
        return (base + SB_SUB - 1 - t >= 0) & (least < SB_DEAD_LOG2)

    def body(carry):
        t, _ = carry
        floor = None
        for a in range(SB_SUB):
            q2 = stacked_q(a)
            for part, back in ((slice(top2, rows2), t), (slice(0, top2), t + 1)):
                kb, rem = earlier_block(a, back, rem_scr[a, part, :])
                z = scores(q2[part], kb, False)
                cum = suffix_mass(z)
                acc_scr[a, part, :] += weighted_values(z, cum, rem, kb)
                rem = rem + cum[:, BLOCK:]
                rem_scr[a, part, :] = rem
                least = jnp.min(rem, axis=0, keepdims=True)
                floor = least if floor is None else jnp.minimum(floor, least)
        return t + 1, jnp.min(floor)

    lax.while_loop(cond, body, (jnp.int32(2), jnp.min(floor)))
    head0_lanes = lax.broadcasted_iota(jnp.int32, (BLOCK, LANES), 1) < HEAD_DIM
    for a in range(SB_SUB):
        head0 = jnp.concatenate([acc_scr[a, :SB_TOP, :], acc_scr[a, top2:top2 + rest, :]], axis=0)
        head1 = jnp.concatenate([acc_scr[a, SB_TOP:top2, :], acc_scr[a, top2 + rest:, :]], axis=0)
        o_ref[0, a * BLOCK:(a + 1) * BLOCK, :] = jnp.where(
            head0_lanes, head0, head1).astype(o_ref.dtype)


def _suffix_sum_matrix():
    j = np.arange(2 * BLOCK)[:, None] % BLOCK
    s = np.arange(2 * BLOCK)[None, :]
    return jnp.asarray(((s >= BLOCK) | (j >= s)).astype(np.float32), dtype=_BF16)


def _sb_attn(qkv, *, q_col, k_col, v_col):
    b, s, _ = qkv.shape
    pairs = SB_HEADS * HEAD_DIM // LANES
    tq = SB_SUB * BLOCK
    assert s % tq == 0
    return pl.pallas_call(
        _sb_kernel,
        grid=(b, pairs, s // tq),
        in_specs=[
            pl.BlockSpec((1, tq, LANES), lambda bi, p, qi: (bi, qi, q_col + p)),
            pl.BlockSpec((1, s, LANES), lambda bi, p, qi: (bi, 0, k_col + p)),
            pl.BlockSpec((1, s, LANES), lambda bi, p, qi: (bi, 0, v_col + p)),
            pl.BlockSpec((2 * BLOCK, 2 * BLOCK), lambda bi, p, qi: (0, 0)),
        ],
        out_specs=pl.BlockSpec((1, tq, LANES), lambda bi, p, qi: (bi, qi, p)),
        out_shape=jax.ShapeDtypeStruct((b, s, SB_HEADS * HEAD_DIM), _BF16),
        scratch_shapes=[
            pltpu.VMEM((SB_SUB, 2 * BLOCK, BLOCK), _F32),
            pltpu.VMEM((SB_SUB, 2 * BLOCK, LANES), _F32),
        ],
        compiler_params=pltpu.CompilerParams(
            dimension_semantics=("arbitrary", "arbitrary", "arbitrary"),
            vmem_limit_bytes=VMEM_LIMIT),
        name="sb_attn",
    )(qkv, qkv, qkv, _suffix_sum_matrix())


def _sw_bias_kernel(bucket_ref, rel_ref, o_ref):
    bucket = bucket_ref[...]
    for h in range(SW_HEADS):
        acc = jnp.zeros(bucket.shape, _F32)
        for b in range(NUM_BUCKETS):
            acc = jnp.where(bucket == b, rel_ref[b, h], acc)
        o_ref[h] = acc


def _t5_bucket(dist):
    max_exact = NUM_BUCKETS // 2
    d = jnp.maximum(dist, 1).astype(_F32)
    large = max_exact + (jnp.log(d / max_exact) / math.log(MAX_DISTANCE / max_exact)
                         * (NUM_BUCKETS - max_exact)).astype(jnp.int32)
    large = jnp.minimum(large, NUM_BUCKETS - 1)
    return jnp.where(dist < max_exact, dist, large)


def _sw_bias(rel_bias):
    dist = (jnp.arange(BLOCK)[:, None] + BLOCK) - jnp.arange(2 * BLOCK)[None, :]
    bucket = _t5_bucket(jnp.maximum(dist, 0)).astype(jnp.int32)
    return pl.pallas_call(
        _sw_bias_kernel,
        in_specs=[
            pl.BlockSpec(memory_space=pltpu.VMEM),
            pl.BlockSpec(memory_space=pltpu.SMEM),
        ],
        out_specs=pl.BlockSpec(memory_space=pltpu.VMEM),
        out_shape=jax.ShapeDtypeStruct((SW_HEADS, BLOCK, 2 * BLOCK), _F32),
        name="sw_bias",
    )(bucket, rel_bias.astype(_F32))


SW_SUB = 8
SW_GROUP = SW_HEADS // SW_KV_HEADS


def _sw_kernel(q_ref, kp_ref, kc_ref, vp_ref, vc_ref, bias_ref, o_ref):
    rows = SW_GROUP * BLOCK
    upper_lanes = lax.broadcasted_iota(jnp.int32, (rows, LANES), 1) >= HEAD_DIM
    sink_key = lax.broadcasted_iota(jnp.int32, (2 * BLOCK, LANES), 0) == 0
    qpos = (lax.broadcasted_iota(jnp.int32, (rows, 2 * BLOCK), 0) & (BLOCK - 1)) + BLOCK
    kpos = lax.broadcasted_iota(jnp.int32, (rows, 2 * BLOCK), 1)
    dist = qpos - kpos
    in_window = (dist >= 0) & (dist < BLOCK)
    first_key = jnp.where(pl.program_id(1) > 0, 0, BLOCK)
    valid_first = (in_window & (kpos >= first_key)) | (kpos == 0)
    valid_later = in_window | (kpos == 0)

    def kv_rows(prev_ref, cur_ref, blk):
        cur = cur_ref[0, blk * BLOCK:(blk + 1) * BLOCK, :]
        prev = prev_ref[0] if blk == 0 else cur_ref[0, (blk - 1) * BLOCK:blk * BLOCK, :]
        kv = jnp.concatenate([prev, cur], axis=0)
        return jnp.where(sink_key, jnp.zeros_like(kv), kv)

    def logits(blk, g):
        q4 = jnp.concatenate(
            [q_ref[0, blk * BLOCK:(blk + 1) * BLOCK, t * LANES:(t + 1) * LANES]
             for t in range(SW_GROUP)], axis=0)
        q4 = jnp.where(upper_lanes if g == 1 else ~upper_lanes, q4, jnp.zeros_like(q4))
        return _dot_nt(q4, kv_rows(kp_ref, kc_ref, blk))

    def exponentials(z, blk, g):
        z = jnp.where(valid_first if blk == 0 else valid_later, z + bias_ref[g], NEG_INF)
        return jnp.exp(z - jnp.max(z, axis=-1, keepdims=True)).astype(_BF16)

    def values(p, blk):
        vv = kv_rows(vp_ref, vc_ref, blk)
        pv = _dot(p, jnp.concatenate([vv, jnp.ones_like(vv)], axis=1))
        return pv[:, :LANES] * (1.0 / pv[:, LANES:])

    items = [(blk, g) for blk in range(SW_SUB) for g in range(SW_KV_HEADS)]
    zs, ps, outs = {}, {}, {}
    for step in range(len(items) + 2):
        if step < len(items):
            zs[step] = logits(*items[step])
        if 0 <= step - 1 < len(items):
            ps[step - 1] = exponentials(zs.pop(step - 1), *items[step - 1])
        if 0 <= step - 2 < len(items):
            blk, g = items[step - 2]
            outs[g] = values(ps.pop(step - 2), blk)
            if g == SW_KV_HEADS - 1:
                y = jnp.where(upper_lanes, outs.pop(1), outs.pop(0)).astype(o_ref.dtype)
                for t in range(SW_GROUP):
                    o_ref[0, blk * BLOCK:(blk + 1) * BLOCK, t * LANES:(t + 1) * LANES] = (
                        y[t * BLOCK:(t + 1) * BLOCK])


def _sw_attn(qkv, sinks, bias, *, q_col, k_col, v_col):
    b, s, _ = qkv.shape
    qw = SW_HEADS * HEAD_DIM
    tq = SW_SUB * BLOCK
    assert s % tq == 0 and SW_KV_HEADS * HEAD_DIM == LANES
    prev = lambda bi, n: (bi, jnp.maximum(n * SW_SUB - 1, 0))
    table = bias.at[:, :, 0].set(sinks.astype(_F32)[:, None]).reshape(
        SW_KV_HEADS, SW_GROUP * BLOCK, 2 * BLOCK)
    return pl.pallas_call(
        _sw_kernel,
        grid=(b, s // tq),
        in_specs=[
            pl.BlockSpec((1, tq, qw), lambda bi, n: (bi, n, q_col)),
            pl.BlockSpec((1, BLOCK, LANES), lambda bi, n: prev(bi, n) + (k_col,)),
            pl.BlockSpec((1, tq, LANES), lambda bi, n: (bi, n, k_col)),
            pl.BlockSpec((1, BLOCK, LANES), lambda bi, n: prev(bi, n) + (v_col,)),
            pl.BlockSpec((1, tq, LANES), lambda bi, n: (bi, n, v_col)),
            pl.BlockSpec((SW_KV_HEADS, SW_GROUP * BLOCK, 2 * BLOCK), lambda bi, n: (0, 0, 0)),
        ],
        out_specs=pl.BlockSpec((1, tq, qw), lambda bi, n: (bi, n, 0)),
        out_shape=jax.ShapeDtypeStruct((b, s, qw), _BF16),
        compiler_params=pltpu.CompilerParams(
            dimension_semantics=("arbitrary", "arbitrary"), vmem_limit_bytes=VMEM_LIMIT),
        name="sw_attn",
    )(qkv, qkv, qkv, qkv, qkv, table)


MIX_CHUNK = 256


def _mix_kernel(x_ref, g_ref, wg_ref, ysb_ref, ysw_ref, wsb_ref, wsw_ref, wo_ref, o_ref,
                h_scr, merged_scr):
    d = x_ref.shape[1]
    tm = x_ref.shape[0] // ROW_SUB
    n_chunks = d // MIX_CHUNK
    rows = lambda s: slice(s * tm, (s + 1) * tm)

    def prologue(s):
        h_scr[rows(s), :] = _rms(x_ref[rows(s), :], g_ref[...]).astype(_BF16)

    def projections(s, c):
        cols = slice(c * MIX_CHUNK, (c + 1) * MIX_CHUNK)
        gate_cols = slice(d + c * MIX_CHUNK, d + (c + 1) * MIX_CHUNK)
        h = h_scr[rows(s), :]
        return (_dot(h, wg_ref[:, cols]), _dot(ysb_ref[rows(s), :], wsb_ref[:, cols]),
                _dot(h, wg_ref[:, gate_cols]), _dot(ysw_ref[rows(s), :], wsw_ref[:, cols]))

    def epilogue(s, y):
        o_ref[rows(s), :] = x_ref[rows(s), :] + y

    prologue(0)
    ahead = projections(0, 0)
    pending = None
    for s in range(ROW_SUB):
        for c in range(n_chunks):
            logit_sb, y_sb, logit_sw, y_sw = ahead
            if c + 1 < n_chunks:
                ahead = projections(s, c + 1)
            elif s + 1 < ROW_SUB:
                ahead = projections(s + 1, 0)
            merged = jax.nn.sigmoid(logit_sb) * y_sb + jax.nn.sigmoid(logit_sw) * y_sw
            merged_scr[rows(s), c * MIX_CHUNK:(c + 1) * MIX_CHUNK] = merged.astype(_BF16)
            if c == 0 and s + 1 < ROW_SUB:
                prologue(s + 1)
            if c == 1 and pending is not None:
                epilogue(*pending)
                pending = None
        pending = (s, _dot(merged_scr[rows(s), :], wo_ref[...]))
    epilogue(*pending)


def _mix_out(x2d, g, w_gate, y_sb, y_sw, w_sb, w_sw, w_out, *, tm):
    n, d = x2d.shape
    assert d % MIX_CHUNK == 0
    step_rows = ROW_SUB * tm
    full = lambda a: pl.BlockSpec(a.shape, lambda i: (0, 0), pipeline_mode=pl.Buffered(1))
    rows = lambda a: pl.BlockSpec((step_rows, a.shape[1]), lambda i: (i, 0))
    return pl.pallas_call(
        _mix_kernel,
        grid=(n // step_rows,),
        in_specs=[rows(x2d), full(g), full(w_gate), rows(y_sb), rows(y_sw),
                  full(w_sb), full(w_sw), full(w_out)],
        out_specs=pl.BlockSpec((step_rows, d), lambda i: (i, 0)),
        out_shape=jax.ShapeDtypeStruct((n, d), _F32),
        scratch_shapes=[pltpu.VMEM((step_rows, d), _BF16), pltpu.VMEM((step_rows, d), _BF16)],
        compiler_params=pltpu.CompilerParams(
            dimension_semantics=("arbitrary",), vmem_limit_bytes=VMEM_LIMIT),
        name="mix_out",
    )(x2d, g, w_gate, y_sb, y_sw, w_sb, w_sw, w_out)


FFN_SUB = 2
HALO = 8


def _ffn_kernel(x_ref, g_ref, wu_ref, cw_ref, cb_ref, wd_ref, gf_ref, o_ref,
                h_scr, act_scr, u_scr, carry_scr, *, steps_per_seq):
    tm = x_ref.shape[0] // FFN_SUB
    d_ff = wd_ref.shape[0]
    nj = d_ff // FF_CHUNK
    rows = lambda s: slice(s * tm, (s + 1) * tm)

    @pl.when(pl.program_id(0) % steps_per_seq == 0)
    def _():
        carry_scr[...] = jnp.zeros_like(carry_scr)

    def prologue(s):
        h_scr[rows(s), :] = _rms(x_ref[rows(s), :], g_ref[...]).astype(_BF16)

    def up_proj(s, j):
        h = h_scr[rows(s), :]
        gate = _dot(h, wu_ref[:, j * FF_CHUNK:(j + 1) * FF_CHUNK])
        value = _dot(h, wu_ref[:, d_ff + j * FF_CHUNK:d_ff + (j + 1) * FF_CHUNK])
        return gate, value

    def conv(slot, j, part, u):
        slabs = []
        for k in range(FF_CHUNK // LANES):
            slab = (slot * 2 + part) * (FF_CHUNK // LANES) + k
            cols = slice(part * d_ff + j * FF_CHUNK + k * LANES,
                         part * d_ff + j * FF_CHUNK + (k + 1) * LANES)
            lanes = slice((part * FF_CHUNK // LANES + k) * LANES,
                          (part * FF_CHUNK // LANES + k + 1) * LANES)
            uk = u[:, k * LANES:(k + 1) * LANES]
            u_scr[slab, HALO:HALO + tm, :] = uk
            u_scr[slab, 0:HALO, :] = carry_scr[j, :, lanes]
            carry_scr[j, :, lanes] = uk[tm - HALO:tm, :]
            y = cb_ref[:, cols] + cw_ref[CONV_WIDTH - 1:CONV_WIDTH, cols] * uk
            for tap in range(CONV_WIDTH - 1):
                back = CONV_WIDTH - 1 - tap
                y = y + cw_ref[tap:tap + 1, cols] * u_scr[slab, pl.ds(HALO - back, tm, stride=1), :]
            slabs.append(y)
        return jnp.concatenate(slabs, axis=1)

    def gated_conv(s, j, gate, value):
        slot = (s * nj + j) % 2
        act = jax.nn.silu(conv(slot, j, 0, gate)) * conv(slot, j, 1, value)
        act_scr[rows(s), j * FF_CHUNK:(j + 1) * FF_CHUNK] = act.astype(_BF16)

    def epilogue(s, y):
        o_ref[rows(s), :] = _rms(x_ref[rows(s), :] + y, gf_ref[...])

    prologue(0)
    ahead = up_proj(0, 0)
    pending = None
    for s in range(FFN_SUB):
        for j in range(nj):
            gate, value = ahead
            if j + 1 < nj:
                ahead = up_proj(s, j + 1)
            elif s + 1 < FFN_SUB:
                ahead = up_proj(s + 1, 0)
            gated_conv(s, j, gate, value)
            if j == 0 and s + 1 < FFN_SUB:
                prologue(s + 1)
            if j == 1 and pending is not None:
                epilogue(*pending)
                pending = None
        pending = (s, _dot(act_scr[rows(s), :], wd_ref[...]))
    epilogue(*pending)


def _conv_ffn(x1, g_ffn, w_up, conv_w, conv_b, w_down, g_final, *, tm, tiles_per_seq):
    n, d = x1.shape
    d_ff = w_down.shape[0]
    nj = d_ff // FF_CHUNK
    assert tiles_per_seq % FFN_SUB == 0
    step_rows = FFN_SUB * tm
    resident = lambda a: pl.BlockSpec(a.shape, lambda i: (0, 0), pipeline_mode=pl.Buffered(1))
    return pl.pallas_call(
        functools.partial(_ffn_kernel, steps_per_seq=tiles_per_seq // FFN_SUB),
        grid=(n // step_rows,),
        in_specs=[
            pl.BlockSpec((step_rows, d), lambda i: (i, 0)),
            resident(g_ffn), resident(w_up), resident(conv_w), resident(conv_b),
            resident(w_down), resident(g_final),
        ],
        out_specs=pl.BlockSpec((step_rows, d), lambda i: (i, 0)),
        out_shape=jax.ShapeDtypeStruct((n, d), _F32),
        scratch_shapes=[
            pltpu.VMEM((step_rows, d), _BF16),
            pltpu.VMEM((step_rows, d_ff), _BF16),
            pltpu.VMEM((2 * 2 * FF_CHUNK // LANES, HALO + tm, LANES), _F32),
            pltpu.VMEM((nj, HALO, 2 * FF_CHUNK), _F32),
        ],
        compiler_params=pltpu.CompilerParams(
            dimension_semantics=("arbitrary",), vmem_limit_bytes=VMEM_LIMIT),
        name="conv_ffn",
    )(x1, g_ffn, w_up, conv_w, conv_b, w_down, g_final)


def kernel(x, g_mix, w_in, w_sb_proj, w_sw_proj, w_out, rel_bias, sinks, g_ffn, w_up,
           conv_w, conv_b, w_down, g_final):
    b, s, d = x.shape
    n = b * s
    d_ff = w_down.shape[0]
    sb_w = SB_HEADS * HEAD_DIM
    sw_qw = SW_HEADS * HEAD_DIM
    sw_kw = SW_KV_HEADS * HEAD_DIM
    assert s % BLOCK == 0 and d_ff % FF_CHUNK == 0
    tm = min(512, s)
    assert s % (max(ROW_SUB, FFN_SUB) * tm) == 0

    qkv_end = 3 * sb_w + sw_qw
    order = np.arange(SW_HEADS).reshape(SW_KV_HEADS, SW_GROUP).T.reshape(-1)
    by_head = lambda a, axis: jnp.take(
        a.reshape(a.shape[:axis] + (SW_HEADS, HEAD_DIM) + a.shape[axis + 1:]), order,
        axis=axis).reshape(a.shape)
    w_qkv = jnp.concatenate(
        [w_in[:, :3 * sb_w], by_head(w_in[:, 3 * sb_w:qkv_end], 1),
         w_in[:, qkv_end:qkv_end + 2 * sw_kw]], axis=1).astype(_BF16)
    w_gate = w_in[:, qkv_end + 2 * sw_kw:].astype(_BF16)
    row = lambda a: a.reshape(1, -1).astype(_F32)

    x2d = x.reshape(n, d)
    chunk = 512
    qkv = _in_proj(x2d, row(g_mix), w_qkv, tm=tm, chunk=chunk,
                   n_scaled=(0, 3 * sb_w // chunk))
    qkv = qkv.reshape(b, s, -1)
    y_sb = _sb_attn(qkv, q_col=0, k_col=sb_w // LANES, v_col=2 * sb_w // LANES)
    bias = _sw_bias(rel_bias)
    y_sw = _sw_attn(qkv, sinks, bias, q_col=3 * sb_w // sw_qw,
                    k_col=qkv_end // LANES, v_col=qkv_end // LANES + 1)
    x1 = _mix_out(x2d, row(g_mix), w_gate, y_sb.reshape(n, sb_w), y_sw.reshape(n, sw_qw),
                  w_sb_proj.astype(_BF16), by_head(w_sw_proj, 0).astype(_BF16),
                  w_out.astype(_BF16), tm=tm)
    out = _conv_ffn(x1, row(g_ffn), w_up.astype(_BF16), conv_w.astype(_F32), row(conv_b),
                    w_down.astype(_BF16), row(g_final), tm=tm, tiles_per_seq=s // tm)
    return out.reshape(b, s, d)
```

```python
import functools
import math

import jax
import jax.numpy as jnp
import numpy as np
from jax import lax
from jax.experimental import pallas as pl
from jax.experimental.pallas import tpu as pltpu

SB_HEADS = 8
SW_HEADS = 8
SW_KV_HEADS = 2
HEAD_DIM = 64
BLOCK = 128
NUM_BUCKETS = 32
MAX_DISTANCE = 128
CONV_WIDTH = 3
EPS = 1e-6
NEG_INF = -1e30
LANES = 128
FF_CHUNK = 256
VMEM_LIMIT = 56 * 1024 * 1024

_BF16 = jnp.bfloat16
_F32 = jnp.float32


def _rms(x, g):
    return x * lax.rsqrt(jnp.mean(x * x, axis=-1, keepdims=True) + EPS) * g


def _dot(a, b):
    return jnp.dot(a, b, preferred_element_type=_F32)


def _dot_nt(a, b):
    return lax.dot_general(a, b, (((1,), (1,)), ((), ())), preferred_element_type=_F32)


ROW_SUB = 2


def _in_proj_kernel(x_ref, g_ref, w_ref, o_ref, h_scr, *, n_scaled, chunk):
    tm = x_ref.shape[0] // ROW_SUB
    rows = lambda s: slice(s * tm, (s + 1) * tm)
    scale = 1.0 / math.sqrt(HEAD_DIM)

    def prologue(s):
        h_scr[rows(s), :] = _rms(x_ref[rows(s), :], g_ref[...]).astype(_BF16)

    prologue(0)
    width = w_ref.shape[1]
    for s in range(ROW_SUB):
        for c, start in enumerate(range(0, width, chunk)):
            cols = slice(start, min(start + chunk, width))
            p = _dot(h_scr[rows(s), :], w_ref[:, cols])
            if c in n_scaled:
                p = p * scale
            o_ref[rows(s), cols] = p.astype(_BF16)
            if c == 0 and s + 1 < ROW_SUB:
                prologue(s + 1)


def _in_proj(x2d, g, w, *, tm, n_scaled, chunk):
    n, d = x2d.shape
    width = w.shape[1]
    step_rows = ROW_SUB * tm
    return pl.pallas_call(
        functools.partial(_in_proj_kernel, n_scaled=n_scaled, chunk=chunk),
        grid=(n // step_rows,),
        in_specs=[
            pl.BlockSpec((step_rows, d), lambda i: (i, 0)),
            pl.BlockSpec((1, d), lambda i: (0, 0), pipeline_mode=pl.Buffered(1)),
            pl.BlockSpec((d, width), lambda i: (0, 0), pipeline_mode=pl.Buffered(1)),
        ],
        out_specs=pl.BlockSpec((step_rows, width), lambda i: (i, 0)),
        out_shape=jax.ShapeDtypeStruct((n, width), _BF16),
        scratch_shapes=[pltpu.VMEM((step_rows, d), _BF16)],
        compiler_params=pltpu.CompilerParams(
            dimension_semantics=("arbitrary",), vmem_limit_bytes=VMEM_LIMIT),
        name="in_proj",
    )(x2d, g, w)


SB_SUB = 32
SB_TOP = 32
SB_MASKED = -1e30
SB_SKEW = 3
SB_OFF = 1e30
LOG2E = 1.4426950408889634
SB_DEAD_LOG2 = 151.0


def _sb_kernel(q_ref, k_ref, v_ref, t_ref, o_ref, rem_scr, acc_scr):
    base = pl.program_id(2) * SB_SUB
    tri = t_ref[...]
    rows2, top2 = 2 * BLOCK, 2 * SB_TOP
    rest = BLOCK - SB_TOP
    lane = lax.broadcasted_iota(jnp.int32, (rows2, LANES), 1)
    row = lax.broadcasted_iota(jnp.int32, (rows2, LANES), 0)
    head1 = ((row >= SB_TOP) & (row < top2)) | (row >= top2 + rest)
    own_lanes = (lane >= HEAD_DIM) == head1
    query = jnp.where(row < SB_TOP, row, jnp.where(row < top2 + rest, row - SB_TOP, row - BLOCK))
    strictly_earlier = lane < query

    def stacked_q(a):
        q = q_ref[0, a * BLOCK:(a + 1) * BLOCK, :]
        q2 = jnp.concatenate([q[:SB_TOP], q[:SB_TOP], q[SB_TOP:], q[SB_TOP:]], axis=0)
        return jnp.where(own_lanes, q2, jnp.zeros_like(q2))

    def key_rows(kb):
        return pl.ds(pl.multiple_of(kb * BLOCK, BLOCK), BLOCK)

    def scores(q2, kb, diagonal):
        z = _dot_nt(q2, k_ref[0, key_rows(kb), :]) * LOG2E
        return jnp.where(strictly_earlier, z, SB_MASKED) if diagonal else z

    def suffix_mass(z):
        sign = jnp.uint32(0x80000000)
        neg_abs = lax.bitcast_convert_type(lax.bitcast_convert_type(z, jnp.uint32) | sign, _F32)
        sp = jnp.maximum(z, 0.0) + jnp.log2(1.0 + jnp.exp2(neg_abs))
        hi = sp.astype(_BF16)
        lo = (sp - hi.astype(_F32)).astype(_BF16)
        return _dot(jnp.concatenate([hi, lo], axis=1), tri)

    def weighted_values(z, cum, rem, kb):
        w = jnp.exp2(z - cum[:, :BLOCK] - rem)
        return _dot(w.astype(_BF16), v_ref[0, key_rows(kb), :])

    def earlier_block(a, t, rem):
        kb = base + a - t
        return jnp.maximum(kb, 0), rem + jnp.where(kb < 0, SB_OFF, 0.0)

    items = [(a, t, slice(0, top2) if t == 2 else slice(0, rows2))
             for a in range(SB_SUB) for t in range(3)]
    q2s, zs, cums = {}, {}, {}
    rems = [jnp.zeros((rows2, BLOCK), _F32) for _ in range(SB_SUB)]
    accs = [jnp.zeros((rows2, LANES), _F32) for _ in range(SB_SUB)]
    for step in range(len(items) + 2 * SB_SKEW):
        if step < len(items):
            a, t, part = items[step]
            if t == 0:
                q2s[a] = stacked_q(a)
            zs[step] = scores(q2s[a][part], jnp.maximum(base + a - t, 0), t == 0)
        i = step - SB_SKEW
        if 0 <= i < len(items):
            cums[i] = suffix_mass(zs[i])
        i = step - 2 * SB_SKEW
        if 0 <= i < len(items):
            a, t, part = items[i]
            kb, rem = earlier_block(a, t, rems[a][part])
            pv = weighted_values(zs.pop(i), cums[i], rem, kb)
            rem = rem + cums.pop(i)[:, BLOCK:]
            acc = accs[a][part] + pv
            if part.stop < rows2:
                rem = jnp.concatenate([rem, rems[a][part.stop:]], axis=0)
                acc = jnp.concatenate([acc, accs[a][part.stop:]], axis=0)
            rems[a], accs[a] = rem, acc
    floor = None
    for a in range(SB_SUB):
        rem_scr[a] = rems[a]
        acc_scr[a] = accs[a]
        floor = rems[a] if floor is None else jnp.minimum(floor, rems[a])

    def cond(carry):
        t, least = carry
        return (base + SB_SUB - 1 - t >= 0) & (least < SB_DEAD_LOG2)

    def body(carry):
        t, _ = carry
        floor = None
        for a in range(SB_SUB):
            q2 = stacked_q(a)
            for part, back in ((slice(top2, rows2), t), (slice(0, top2), t + 1)):
                kb, rem = earlier_block(a, back, rem_scr[a, part, :])
                z = scores(q2[part], kb, False)
                cum = suffix_mass(z)
                acc_scr[a, part, :] += weighted_values(z, cum, rem, kb)
                rem = rem + cum[:, BLOCK:]
                rem_scr[a, part, :] = rem
                least = jnp.min(rem, axis=0, keepdims=True)
                floor = least if floor is None else jnp.minimum(floor, least)
        return t + 1, jnp.min(floor)

    lax.while_loop(cond, body, (jnp.int32(2), jnp.min(floor)))
    head0_lanes = lax.broadcasted_iota(jnp.int32, (BLOCK, LANES), 1) < HEAD_DIM
    for a in range(SB_SUB):
        head0 = jnp.concatenate([acc_scr[a, :SB_TOP, :], acc_scr[a, top2:top2 + rest, :]], axis=0)
        head1 = jnp.concatenate([acc_scr[a, SB_TOP:top2, :], acc_scr[a, top2 + rest:, :]], axis=0)
        o_ref[0, a * BLOCK:(a + 1) * BLOCK, :] = jnp.where(
            head0_lanes, head0, head1).astype(o_ref.dtype)


def _suffix_sum_matrix():
    j = np.arange(2 * BLOCK)[:, None] % BLOCK
    s = np.arange(2 * BLOCK)[None, :]
    return jnp.asarray(((s >= BLOCK) | (j >= s)).astype(np.float32), dtype=_BF16)


def _sb_attn(qkv, *, q_col, k_col, v_col):
    b, s, _ = qkv.shape
    pairs = SB_HEADS * HEAD_DIM // LANES
    tq = SB_SUB * BLOCK
    assert s % tq == 0
    return pl.pallas_call(
        _sb_kernel,
        grid=(b, pairs, s // tq),
        in_specs=[
            pl.BlockSpec((1, tq, LANES), lambda bi, p, qi: (bi, qi, q_col + p)),
            pl.BlockSpec((1, s, LANES), lambda bi, p, qi: (bi, 0, k_col + p)),
            pl.BlockSpec((1, s, LANES), lambda bi, p, qi: (bi, 0, v_col + p)),
            pl.BlockSpec((2 * BLOCK, 2 * BLOCK), lambda bi, p, qi: (0, 0)),
        ],
        out_specs=pl.BlockSpec((1, tq, LANES), lambda bi, p, qi: (bi, qi, p)),
        out_shape=jax.ShapeDtypeStruct((b, s, SB_HEADS * HEAD_DIM), _BF16),
        scratch_shapes=[
            pltpu.VMEM((SB_SUB, 2 * BLOCK, BLOCK), _F32),
            pltpu.VMEM((SB_SUB, 2 * BLOCK, LANES), _F32),
        ],
        compiler_params=pltpu.CompilerParams(
            dimension_semantics=("arbitrary", "arbitrary", "arbitrary"),
            vmem_limit_bytes=VMEM_LIMIT),
        name="sb_attn",
    )(qkv, qkv, qkv, _suffix_sum_matrix())


def _sw_bias_kernel(bucket_ref, rel_ref, sink_ref, o_ref):
    bucket = bucket_ref[...]
    sink_col = lax.broadcasted_iota(jnp.int32, bucket.shape, 1) == 0
    for h in range(SW_HEADS):
        acc = jnp.zeros(bucket.shape, _F32)
        for b in range(NUM_BUCKETS):
            acc = jnp.where(bucket == b, rel_ref[b, h], acc)
        o_ref[h // SW_GROUP, (h % SW_GROUP) * BLOCK:(h % SW_GROUP + 1) * BLOCK, :] = jnp.where(
            sink_col, sink_ref[h], acc)


def _t5_bucket(dist):
    max_exact = NUM_BUCKETS // 2
    d = np.maximum(dist, 1).astype(np.float32)
    large = max_exact + (np.log(d / max_exact) / math.log(MAX_DISTANCE / max_exact)
                         * (NUM_BUCKETS - max_exact)).astype(np.int32)
    large = np.minimum(large, NUM_BUCKETS - 1)
    return np.where(dist < max_exact, dist, large).astype(np.int32)


def _sw_bias(rel_bias, sinks):
    dist = (np.arange(BLOCK)[:, None] + BLOCK) - np.arange(2 * BLOCK)[None, :]
    bucket = jnp.asarray(_t5_bucket(np.maximum(dist, 0)))
    return pl.pallas_call(
        _sw_bias_kernel,
        in_specs=[
            pl.BlockSpec(memory_space=pltpu.VMEM),
            pl.BlockSpec(memory_space=pltpu.SMEM),
            pl.BlockSpec(memory_space=pltpu.SMEM),
        ],
        out_specs=pl.BlockSpec(memory_space=pltpu.VMEM),
        out_shape=jax.ShapeDtypeStruct((SW_KV_HEADS, SW_GROUP * BLOCK, 2 * BLOCK), _F32),
        name="sw_bias",
    )(bucket, rel_bias.astype(_F32), sinks.astype(_F32))


SW_SUB = 8
SW_GROUP = SW_HEADS // SW_KV_HEADS


def _sw_kernel(q_ref, kp_ref, kc_ref, vp_ref, vc_ref, bias_ref, o_ref):
    rows = SW_GROUP * BLOCK
    upper_lanes = lax.broadcasted_iota(jnp.int32, (rows, LANES), 1) >= HEAD_DIM
    sink_key = lax.broadcasted_iota(jnp.int32, (2 * BLOCK, LANES), 0) == 0
    qpos = (lax.broadcasted_iota(jnp.int32, (rows, 2 * BLOCK), 0) & (BLOCK - 1)) + BLOCK
    kpos = lax.broadcasted_iota(jnp.int32, (rows, 2 * BLOCK), 1)
    dist = qpos - kpos
    in_window = (dist >= 0) & (dist < BLOCK)
    first_key = jnp.where(pl.program_id(1) > 0, 0, BLOCK)
    valid_first = (in_window & (kpos >= first_key)) | (kpos == 0)
    valid_later = in_window | (kpos == 0)

    def kv_rows(prev_ref, cur_ref, blk):
        cur = cur_ref[0, blk * BLOCK:(blk + 1) * BLOCK, :]
        prev = prev_ref[0] if blk == 0 else cur_ref[0, (blk - 1) * BLOCK:blk * BLOCK, :]
        kv = jnp.concatenate([prev, cur], axis=0)
        return jnp.where(sink_key, jnp.zeros_like(kv), kv)

    def logits(blk, g):
        q4 = jnp.concatenate(
            [q_ref[0, blk * BLOCK:(blk + 1) * BLOCK, t * LANES:(t + 1) * LANES]
             for t in range(SW_GROUP)], axis=0)
        q4 = jnp.where(upper_lanes if g == 1 else ~upper_lanes, q4, jnp.zeros_like(q4))
        return _dot_nt(q4, kv_rows(kp_ref, kc_ref, blk))

    def exponentials(z, blk, g):
        z = jnp.where(valid_first if blk == 0 else valid_later, z + bias_ref[g], NEG_INF)
        return jnp.exp(z - jnp.max(z, axis=-1, keepdims=True)).astype(_BF16)

    def values(p, blk):
        vv = kv_rows(vp_ref, vc_ref, blk)
        pv = _dot(p, jnp.concatenate([vv, jnp.ones_like(vv)], axis=1))
        return pv[:, :LANES] * (1.0 / pv[:, LANES:])

    items = [(blk, g) for blk in range(SW_SUB) for g in range(SW_KV_HEADS)]
    zs, ps, outs = {}, {}, {}
    for step in range(len(items) + 2):
        if step < len(items):
            zs[step] = logits(*items[step])
        if 0 <= step - 1 < len(items):
            ps[step - 1] = exponentials(zs.pop(step - 1), *items[step - 1])
        if 0 <= step - 2 < len(items):
            blk, g = items[step - 2]
            outs[g] = values(ps.pop(step - 2), blk)
            if g == SW_KV_HEADS - 1:
                y = jnp.where(upper_lanes, outs.pop(1), outs.pop(0)).astype(o_ref.dtype)
                for t in range(SW_GROUP):
                    o_ref[0, blk * BLOCK:(blk + 1) * BLOCK, t * LANES:(t + 1) * LANES] = (
                        y[t * BLOCK:(t + 1) * BLOCK])


def _sw_attn(qkv, table, *, q_col, k_col, v_col):
    b, s, _ = qkv.shape
    qw = SW_HEADS * HEAD_DIM
    tq = SW_SUB * BLOCK
    assert s % tq == 0 and SW_KV_HEADS * HEAD_DIM == LANES
    prev = lambda bi, n: (bi, jnp.maximum(n * SW_SUB - 1, 0))
    return pl.pallas_call(
        _sw_kernel,
        grid=(b, s // tq),
        in_specs=[
            pl.BlockSpec((1, tq, qw), lambda bi, n: (bi, n, q_col)),
            pl.BlockSpec((1, BLOCK, LANES), lambda bi, n: prev(bi, n) + (k_col,)),
            pl.BlockSpec((1, tq, LANES), lambda bi, n: (bi, n, k_col)),
            pl.BlockSpec((1, BLOCK, LANES), lambda bi, n: prev(bi, n) + (v_col,)),
            pl.BlockSpec((1, tq, LANES), lambda bi, n: (bi, n, v_col)),
            pl.BlockSpec((SW_KV_HEADS, SW_GROUP * BLOCK, 2 * BLOCK), lambda bi, n: (0, 0, 0)),
        ],
        out_specs=pl.BlockSpec((1, tq, qw), lambda bi, n: (bi, n, 0)),
        out_shape=jax.ShapeDtypeStruct((b, s, qw), _BF16),
        compiler_params=pltpu.CompilerParams(
            dimension_semantics=("arbitrary", "arbitrary"), vmem_limit_bytes=VMEM_LIMIT),
        name="sw_attn",
    )(qkv, qkv, qkv, qkv, qkv, table)


MIX_CHUNK = 256


def _mix_kernel(x_ref, g_ref, wg_ref, ysb_ref, ysw_ref, wsb_ref, wsw_ref, wo_ref, o_ref,
                h_scr, merged_scr):
    d = x_ref.shape[1]
    tm = x_ref.shape[0] // ROW_SUB
    n_chunks = d // MIX_CHUNK
    rows = lambda s: slice(s * tm, (s + 1) * tm)

    def prologue(s):
        h_scr[rows(s), :] = _rms(x_ref[rows(s), :], g_ref[...]).astype(_BF16)

    def projections(s, c):
        cols = slice(c * MIX_CHUNK, (c + 1) * MIX_CHUNK)
        gate_cols = slice(d + c * MIX_CHUNK, d + (c + 1) * MIX_CHUNK)
        h = h_scr[rows(s), :]
        return (_dot(h, wg_ref[:, cols]), _dot(ysb_ref[rows(s), :], wsb_ref[:, cols]),
                _dot(h, wg_ref[:, gate_cols]), _dot(ysw_ref[rows(s), :], wsw_ref[:, cols]))

    def epilogue(s, y):
        o_ref[rows(s), :] = x_ref[rows(s), :] + y

    prologue(0)
    ahead = projections(0, 0)
    pending = None
    for s in range(ROW_SUB):
        for c in range(n_chunks):
            logit_sb, y_sb, logit_sw, y_sw = ahead
            if c + 1 < n_chunks:
                ahead = projections(s, c + 1)
            elif s + 1 < ROW_SUB:
                ahead = projections(s + 1, 0)
            merged = jax.nn.sigmoid(logit_sb) * y_sb + jax.nn.sigmoid(logit_sw) * y_sw
            merged_scr[rows(s), c * MIX_CHUNK:(c + 1) * MIX_CHUNK] = merged.astype(_BF16)
            if c == 0 and s + 1 < ROW_SUB:
                prologue(s + 1)
            if c == 1 and pending is not None:
                epilogue(*pending)
                pending = None
        pending = (s, _dot(merged_scr[rows(s), :], wo_ref[...]))
    epilogue(*pending)


def _mix_out(x2d, g, w_gate, y_sb, y_sw, w_sb, w_sw, w_out, *, tm):
    n, d = x2d.shape
    assert d % MIX_CHUNK == 0
    step_rows = ROW_SUB * tm
    full = lambda a: pl.BlockSpec(a.shape, lambda i: (0, 0), pipeline_mode=pl.Buffered(1))
    rows = lambda a: pl.BlockSpec((step_rows, a.shape[1]), lambda i: (i, 0))
    return pl.pallas_call(
        _mix_kernel,
        grid=(n // step_rows,),
        in_specs=[rows(x2d), full(g), full(w_gate), rows(y_sb), rows(y_sw),
                  full(w_sb), full(w_sw), full(w_out)],
        out_specs=pl.BlockSpec((step_rows, d), lambda i: (i, 0)),
        out_shape=jax.ShapeDtypeStruct((n, d), _F32),
        scratch_shapes=[pltpu.VMEM((step_rows, d), _BF16), pltpu.VMEM((step_rows, d), _BF16)],
        compiler_params=pltpu.CompilerParams(
            dimension_semantics=("arbitrary",), vmem_limit_bytes=VMEM_LIMIT),
        name="mix_out",
    )(x2d, g, w_gate, y_sb, y_sw, w_sb, w_sw, w_out)


FFN_SUB = 2
HALO = 8


def _ffn_kernel(x_ref, g_ref, wu_ref, cw_ref, cb_ref, wd_ref, gf_ref, o_ref,
                h_scr, act_scr, u_scr, carry_scr, *, steps_per_seq):
    tm = x_ref.shape[0] // FFN_SUB
    d_ff = wd_ref.shape[0]
    nj = d_ff // FF_CHUNK
    rows = lambda s: slice(s * tm, (s + 1) * tm)

    @pl.when(pl.program_id(0) % steps_per_seq == 0)
    def _():
        carry_scr[...] = jnp.zeros_like(carry_scr)

    def prologue(s):
        h_scr[rows(s), :] = _rms(x_ref[rows(s), :], g_ref[...]).astype(_BF16)

    def up_proj(s, j):
        h = h_scr[rows(s), :]
        gate = _dot(h, wu_ref[:, j * FF_CHUNK:(j + 1) * FF_CHUNK])
        value = _dot(h, wu_ref[:, d_ff + j * FF_CHUNK:d_ff + (j + 1) * FF_CHUNK])
        return gate, value

    def conv(slot, j, part, u):
        slabs = []
        for k in range(FF_CHUNK // LANES):
            slab = (slot * 2 + part) * (FF_CHUNK // LANES) + k
            cols = slice(part * d_ff + j * FF_CHUNK + k * LANES,
                         part * d_ff + j * FF_CHUNK + (k + 1) * LANES)
            lanes = slice((part * FF_CHUNK // LANES + k) * LANES,
                          (part * FF_CHUNK // LANES + k + 1) * LANES)
            uk = u[:, k * LANES:(k + 1) * LANES]
            u_scr[slab, HALO:HALO + tm, :] = uk
            u_scr[slab, 0:HALO, :] = carry_scr[j, :, lanes]
            carry_scr[j, :, lanes] = uk[tm - HALO:tm, :]
            y = cb_ref[:, cols] + cw_ref[CONV_WIDTH - 1:CONV_WIDTH, cols] * uk
            for tap in range(CONV_WIDTH - 1):
                back = CONV_WIDTH - 1 - tap
                y = y + cw_ref[tap:tap + 1, cols] * u_scr[slab, pl.ds(HALO - back, tm, stride=1), :]
            slabs.append(y)
        return jnp.concatenate(slabs, axis=1)

    def gated_conv(s, j, gate, value):
        slot = (s * nj + j) % 2
        act = jax.nn.silu(conv(slot, j, 0, gate)) * conv(slot, j, 1, value)
        act_scr[rows(s), j * FF_CHUNK:(j + 1) * FF_CHUNK] = act.astype(_BF16)

    def epilogue(s, y):
        o_ref[rows(s), :] = _rms(x_ref[rows(s), :] + y, gf_ref[...])

    prologue(0)
    ahead = up_proj(0, 0)
    pending = None
    for s in range(FFN_SUB):
        for j in range(nj):
            gate, value = ahead
            if j + 1 < nj:
                ahead = up_proj(s, j + 1)
            elif s + 1 < FFN_SUB:
                ahead = up_proj(s + 1, 0)
            gated_conv(s, j, gate, value)
            if j == 0 and s + 1 < FFN_SUB:
                prologue(s + 1)
            if j == 1 and pending is not None:
                epilogue(*pending)
                pending = None
        pending = (s, _dot(act_scr[rows(s), :], wd_ref[...]))
    epilogue(*pending)


def _conv_ffn(x1, g_ffn, w_up, conv_w, conv_b, w_down, g_final, *, tm, tiles_per_seq):
    n, d = x1.shape
    d_ff = w_down.shape[0]
    nj = d_ff // FF_CHUNK
    assert tiles_per_seq % FFN_SUB == 0
    step_rows = FFN_SUB * tm
    resident = lambda a: pl.BlockSpec(a.shape, lambda i: (0, 0), pipeline_mode=pl.Buffered(1))
    return pl.pallas_call(
        functools.partial(_ffn_kernel, steps_per_seq=tiles_per_seq // FFN_SUB),
        grid=(n // step_rows,),
        in_specs=[
            pl.BlockSpec((step_rows, d), lambda i: (i, 0)),
            resident(g_ffn), resident(w_up), resident(conv_w), resident(conv_b),
            resident(w_down), resident(g_final),
        ],
        out_specs=pl.BlockSpec((step_rows, d), lambda i: (i, 0)),
        out_shape=jax.ShapeDtypeStruct((n, d), _F32),
        scratch_shapes=[
            pltpu.VMEM((step_rows, d), _BF16),
            pltpu.VMEM((step_rows, d_ff), _BF16),
            pltpu.VMEM((2 * 2 * FF_CHUNK // LANES, HALO + tm, LANES), _F32),
            pltpu.VMEM((nj, HALO, 2 * FF_CHUNK), _F32),
        ],
        compiler_params=pltpu.CompilerParams(
            dimension_semantics=("arbitrary",), vmem_limit_bytes=VMEM_LIMIT),
        name="conv_ffn",
    )(x1, g_ffn, w_up, conv_w, conv_b, w_down, g_final)


def kernel(x, g_mix, w_in, w_sb_proj, w_sw_proj, w_out, rel_bias, sinks, g_ffn, w_up,
           conv_w, conv_b, w_down, g_final):
    b, s, d = x.shape
    n = b * s
    d_ff = w_down.shape[0]
    sb_w = SB_HEADS * HEAD_DIM
    sw_qw = SW_HEADS * HEAD_DIM
    sw_kw = SW_KV_HEADS * HEAD_DIM
    assert s % BLOCK == 0 and d_ff % FF_CHUNK == 0
    tm = min(512, s)
    assert s % (max(ROW_SUB, FFN_SUB) * tm) == 0

    qkv_end = 3 * sb_w + sw_qw
    order = np.arange(SW_HEADS).reshape(SW_KV_HEADS, SW_GROUP).T.reshape(-1)
    by_head = lambda a, axis: jnp.take(
        a.reshape(a.shape[:axis] + (SW_HEADS, HEAD_DIM) + a.shape[axis + 1:]), order,
        axis=axis).reshape(a.shape)
    w_in = w_in.astype(_BF16)
    w_qkv = jnp.concatenate(
        [w_in[:, :3 * sb_w], by_head(w_in[:, 3 * sb_w:qkv_end], 1),
         w_in[:, qkv_end:qkv_end + 2 * sw_kw]], axis=1)
    w_gate = w_in[:, qkv_end + 2 * sw_kw:]
    row = lambda a: a.reshape(1, -1).astype(_F32)

    x2d = x.reshape(n, d)
    chunk = 512
    qkv = _in_proj(x2d, row(g_mix), w_qkv, tm=tm, chunk=chunk,
                   n_scaled=(0, 3 * sb_w // chunk))
    qkv = qkv.reshape(b, s, -1)
    y_sb = _sb_attn(qkv, q_col=0, k_col=sb_w // LANES, v_col=2 * sb_w // LANES)
    y_sw = _sw_attn(qkv, _sw_bias(rel_bias, sinks), q_col=3 * sb_w // sw_qw,
                    k_col=qkv_end // LANES, v_col=qkv_end // LANES + 1)
    x1 = _mix_out(x2d, row(g_mix), w_gate, y_sb.reshape(n, sb_w), y_sw.reshape(n, sw_qw),
                  w_sb_proj.astype(_BF16), by_head(w_sw_proj, 0).astype(_BF16),
                  w_out.astype(_BF16), tm=tm)
    out = _conv_ffn(x1, row(g_ffn), w_up.astype(_BF16), conv_w.astype(_F32), row(conv_b),
                    w_down.astype(_BF16), row(g_final), tm=tm, tiles_per_seq=s // tm)
    return out.reshape(b, s, d)
```

```python
import functools
import math

import jax
import jax.numpy as jnp
import numpy as np
from jax import lax
from jax.experimental import pallas as pl
from jax.experimental.pallas import tpu as pltpu

SB_HEADS = 8
SW_HEADS = 8
SW_KV_HEADS = 2
HEAD_DIM = 64
BLOCK = 128
NUM_BUCKETS = 32
MAX_DISTANCE = 128
CONV_WIDTH = 3
EPS = 1e-6
NEG_INF = -1e30
LANES = 128
V7X_MXU_DIM = 256
V7X_VMEM_BYTES = 64 * 1024 * 1024
VMEM_LIMIT = V7X_VMEM_BYTES * 7 // 8
ROW_TILE = 512
IN_CHUNK = 2 * V7X_MXU_DIM
FF_CHUNK = V7X_MXU_DIM

_BF16 = jnp.bfloat16
_F32 = jnp.float32


def _rms(x, g):
    return x * lax.rsqrt(jnp.mean(x * x, axis=-1, keepdims=True) + EPS) * g


def _dot(a, b):
    return jnp.dot(a, b, preferred_element_type=_F32)


def _dot_nt(a, b):
    return lax.dot_general(a, b, (((1,), (1,)), ((), ())), preferred_element_type=_F32)


ROW_SUB = 2


def _in_proj_kernel(x_ref, g_ref, w_ref, o_ref, h_scr, *, n_scaled, chunk):
    tm = x_ref.shape[0] // ROW_SUB
    rows = lambda s: slice(s * tm, (s + 1) * tm)
    scale = 1.0 / math.sqrt(HEAD_DIM)

    def prologue(s):
        h_scr[rows(s), :] = _rms(x_ref[rows(s), :], g_ref[...]).astype(_BF16)

    prologue(0)
    width = w_ref.shape[1]
    for s in range(ROW_SUB):
        for c, start in enumerate(range(0, width, chunk)):
            cols = slice(start, min(start + chunk, width))
            p = _dot(h_scr[rows(s), :], w_ref[:, cols])
            if c in n_scaled:
                p = p * scale
            o_ref[rows(s), cols] = p.astype(_BF16)
            if c == 0 and s + 1 < ROW_SUB:
                prologue(s + 1)


def _in_proj(x2d, g, w, *, tm, n_scaled, chunk):
    n, d = x2d.shape
    width = w.shape[1]
    step_rows = ROW_SUB * tm
    return pl.pallas_call(
        functools.partial(_in_proj_kernel, n_scaled=n_scaled, chunk=chunk),
        grid=(n // step_rows,),
        in_specs=[
            pl.BlockSpec((step_rows, d), lambda i: (i, 0)),
            pl.BlockSpec((1, d), lambda i: (0, 0), pipeline_mode=pl.Buffered(1)),
            pl.BlockSpec((d, width), lambda i: (0, 0), pipeline_mode=pl.Buffered(1)),
        ],
        out_specs=pl.BlockSpec((step_rows, width), lambda i: (i, 0)),
        out_shape=jax.ShapeDtypeStruct((n, width), _BF16),
        scratch_shapes=[pltpu.VMEM((step_rows, d), _BF16)],
        compiler_params=pltpu.CompilerParams(
            dimension_semantics=("arbitrary",), vmem_limit_bytes=VMEM_LIMIT),
        name="in_proj",
    )(x2d, g, w)


SB_SUB = 32
SB_TOP = 32
SB_MASKED = -1e30
SB_SKEW = 3
SB_OFF = 1e30
LOG2E = 1.4426950408889634
SB_DEAD_LOG2 = 151.0


def _sb_kernel(q_ref, k_ref, v_ref, t_ref, o_ref, rem_scr, acc_scr):
    base = pl.program_id(2) * SB_SUB
    tri = t_ref[...]
    rows2, top2 = 2 * BLOCK, 2 * SB_TOP
    rest = BLOCK - SB_TOP
    lane = lax.broadcasted_iota(jnp.int32, (rows2, LANES), 1)
    row = lax.broadcasted_iota(jnp.int32, (rows2, LANES), 0)
    head1 = ((row >= SB_TOP) & (row < top2)) | (row >= top2 + rest)
    own_lanes = (lane >= HEAD_DIM) == head1
    query = jnp.where(row < SB_TOP, row, jnp.where(row < top2 + rest, row - SB_TOP, row - BLOCK))
    strictly_earlier = lane < query

    def stacked_q(a):
        q = q_ref[0, a * BLOCK:(a + 1) * BLOCK, :]
        q2 = jnp.concatenate([q[:SB_TOP], q[:SB_TOP], q[SB_TOP:], q[SB_TOP:]], axis=0)
        return jnp.where(own_lanes, q2, jnp.zeros_like(q2))

    def key_rows(kb):
        return pl.ds(pl.multiple_of(kb * BLOCK, BLOCK), BLOCK)

    def scores(q2, kb, diagonal):
        z = _dot_nt(q2, k_ref[0, key_rows(kb), :]) * LOG2E
        return jnp.where(strictly_earlier, z, SB_MASKED) if diagonal else z

    def suffix_mass(z):
        sign = jnp.uint32(0x80000000)
        neg_abs = lax.bitcast_convert_type(lax.bitcast_convert_type(z, jnp.uint32) | sign, _F32)
        sp = jnp.maximum(z, 0.0) + jnp.log2(1.0 + jnp.exp2(neg_abs))
        hi = sp.astype(_BF16)
        lo = (sp - hi.astype(_F32)).astype(_BF16)
        return _dot(jnp.concatenate([hi, lo], axis=1), tri)

    def weighted_values(z, cum, rem, kb):
        w = jnp.exp2(z - cum[:, :BLOCK] - rem)
        return _dot(w.astype(_BF16), v_ref[0, key_rows(kb), :])

    def earlier_block(a, t, rem):
        kb = base + a - t
        return jnp.maximum(kb, 0), rem + jnp.where(kb < 0, SB_OFF, 0.0)

    items = [(a, t, slice(0, top2) if t == 2 else slice(0, rows2))
             for a in range(SB_SUB) for t in range(3)]
    q2s, zs, cums = {}, {}, {}
    rems = [jnp.zeros((rows2, BLOCK), _F32) for _ in range(SB_SUB)]
    accs = [jnp.zeros((rows2, LANES), _F32) for _ in range(SB_SUB)]
    for step in range(len(items) + 2 * SB_SKEW):
        if step < len(items):
            a, t, part = items[step]
            if t == 0:
                q2s[a] = stacked_q(a)
            zs[step] = scores(q2s[a][part], jnp.maximum(base + a - t, 0), t == 0)
        i = step - SB_SKEW
        if 0 <= i < len(items):
            cums[i] = suffix_mass(zs[i])
        i = step - 2 * SB_SKEW
        if 0 <= i < len(items):
            a, t, part = items[i]
            kb, rem = earlier_block(a, t, rems[a][part])
            pv = weighted_values(zs.pop(i), cums[i], rem, kb)
            rem = rem + cums.pop(i)[:, BLOCK:]
            acc = accs[a][part] + pv
            if part.stop < rows2:
                rem = jnp.concatenate([rem, rems[a][part.stop:]], axis=0)
                acc = jnp.concatenate([acc, accs[a][part.stop:]], axis=0)
            rems[a], accs[a] = rem, acc
    floor = None
    for a in range(SB_SUB):
        rem_scr[a] = rems[a]
        acc_scr[a] = accs[a]
        floor = rems[a] if floor is None else jnp.minimum(floor, rems[a])

    def cond(carry):
        t, least = carry
        return (base + SB_SUB - 1 - t >= 0) & (least < SB_DEAD_LOG2)

    def body(carry):
        t, _ = carry
        floor = None
        for a in range(SB_SUB):
            q2 = stacked_q(a)
            for part, back in ((slice(top2, rows2), t), (slice(0, top2), t + 1)):
                kb, rem = earlier_block(a, back, rem_scr[a, part, :])
                z = scores(q2[part], kb, False)
                cum = suffix_mass(z)
                acc_scr[a, part, :] += weighted_values(z, cum, rem, kb)
                rem = rem + cum[:, BLOCK:]
                rem_scr[a, part, :] = rem
                least = jnp.min(rem, axis=0, keepdims=True)
                floor = least if floor is None else jnp.minimum(floor, least)
        return t + 1, jnp.min(floor)

    lax.while_loop(cond, body, (jnp.int32(2), jnp.min(floor)))
    head0_lanes = lax.broadcasted_iota(jnp.int32, (BLOCK, LANES), 1) < HEAD_DIM
    for a in range(SB_SUB):
        head0 = jnp.concatenate([acc_scr[a, :SB_TOP, :], acc_scr[a, top2:top2 + rest, :]], axis=0)
        head1 = jnp.concatenate([acc_scr[a, SB_TOP:top2, :], acc_scr[a, top2 + rest:, :]], axis=0)
        o_ref[0, a * BLOCK:(a + 1) * BLOCK, :] = jnp.where(
            head0_lanes, head0, head1).astype(o_ref.dtype)


def _suffix_sum_matrix():
    j = np.arange(2 * BLOCK)[:, None] % BLOCK
    s = np.arange(2 * BLOCK)[None, :]
    return jnp.asarray(((s >= BLOCK) | (j >= s)).astype(np.float32), dtype=_BF16)


def _sb_attn(qkv, *, q_col, k_col, v_col):
    b, s, _ = qkv.shape
    pairs = SB_HEADS * HEAD_DIM // LANES
    tq = SB_SUB * BLOCK
    assert s % tq == 0
    return pl.pallas_call(
        _sb_kernel,
        grid=(b, pairs, s // tq),
        in_specs=[
            pl.BlockSpec((1, tq, LANES), lambda bi, p, qi: (bi, qi, q_col + p)),
            pl.BlockSpec((1, s, LANES), lambda bi, p, qi: (bi, 0, k_col + p)),
            pl.BlockSpec((1, s, LANES), lambda bi, p, qi: (bi, 0, v_col + p)),
            pl.BlockSpec((2 * BLOCK, 2 * BLOCK), lambda bi, p, qi: (0, 0)),
        ],
        out_specs=pl.BlockSpec((1, tq, LANES), lambda bi, p, qi: (bi, qi, p)),
        out_shape=jax.ShapeDtypeStruct((b, s, SB_HEADS * HEAD_DIM), _BF16),
        scratch_shapes=[
            pltpu.VMEM((SB_SUB, 2 * BLOCK, BLOCK), _F32),
            pltpu.VMEM((SB_SUB, 2 * BLOCK, LANES), _F32),
        ],
        compiler_params=pltpu.CompilerParams(
            dimension_semantics=("arbitrary", "arbitrary", "arbitrary"),
            vmem_limit_bytes=VMEM_LIMIT),
        name="sb_attn",
    )(qkv, qkv, qkv, _suffix_sum_matrix())


def _sw_bias_kernel(bucket_ref, rel_ref, sink_ref, o_ref):
    bucket = bucket_ref[...]
    sink_col = lax.broadcasted_iota(jnp.int32, bucket.shape, 1) == 0
    for h in range(SW_HEADS):
        acc = jnp.zeros(bucket.shape, _F32)
        for b in range(NUM_BUCKETS):
            acc = jnp.where(bucket == b, rel_ref[b, h], acc)
        o_ref[h // SW_GROUP, (h % SW_GROUP) * BLOCK:(h % SW_GROUP + 1) * BLOCK, :] = jnp.where(
            sink_col, sink_ref[h], acc)


def _t5_bucket(dist):
    max_exact = NUM_BUCKETS // 2
    d = np.maximum(dist, 1).astype(np.float32)
    large = max_exact + (np.log(d / max_exact) / math.log(MAX_DISTANCE / max_exact)
                         * (NUM_BUCKETS - max_exact)).astype(np.int32)
    large = np.minimum(large, NUM_BUCKETS - 1)
    return np.where(dist < max_exact, dist, large).astype(np.int32)


def _sw_bias(rel_bias, sinks):
    dist = (np.arange(BLOCK)[:, None] + BLOCK) - np.arange(2 * BLOCK)[None, :]
    bucket = jnp.asarray(_t5_bucket(np.maximum(dist, 0)))
    return pl.pallas_call(
        _sw_bias_kernel,
        in_specs=[
            pl.BlockSpec(memory_space=pltpu.VMEM),
            pl.BlockSpec(memory_space=pltpu.SMEM),
            pl.BlockSpec(memory_space=pltpu.SMEM),
        ],
        out_specs=pl.BlockSpec(memory_space=pltpu.VMEM),
        out_shape=jax.ShapeDtypeStruct((SW_KV_HEADS, SW_GROUP * BLOCK, 2 * BLOCK), _F32),
        name="sw_bias",
    )(bucket, rel_bias.astype(_F32), sinks.astype(_F32))


SW_SUB = 32
SW_GROUP = SW_HEADS // SW_KV_HEADS


def _sw_kernel(q_ref, kp_ref, kc_ref, vp_ref, vc_ref, bias_ref, o_ref):
    rows = SW_GROUP * BLOCK
    upper_lanes = lax.broadcasted_iota(jnp.int32, (rows, LANES), 1) >= HEAD_DIM
    sink_key = lax.broadcasted_iota(jnp.int32, (2 * BLOCK, LANES), 0) == 0
    qpos = (lax.broadcasted_iota(jnp.int32, (rows, 2 * BLOCK), 0) & (BLOCK - 1)) + BLOCK
    kpos = lax.broadcasted_iota(jnp.int32, (rows, 2 * BLOCK), 1)
    dist = qpos - kpos
    in_window = (dist >= 0) & (dist < BLOCK)
    first_key = jnp.where(pl.program_id(1) > 0, 0, BLOCK)
    valid_first = (in_window & (kpos >= first_key)) | (kpos == 0)
    valid_later = in_window | (kpos == 0)

    def kv_rows(prev_ref, cur_ref, blk):
        cur = cur_ref[0, blk * BLOCK:(blk + 1) * BLOCK, :]
        prev = prev_ref[0] if blk == 0 else cur_ref[0, (blk - 1) * BLOCK:blk * BLOCK, :]
        kv = jnp.concatenate([prev, cur], axis=0)
        return jnp.where(sink_key, jnp.zeros_like(kv), kv)

    def logits(blk, g):
        q4 = jnp.concatenate(
            [q_ref[0, blk * BLOCK:(blk + 1) * BLOCK, t * LANES:(t + 1) * LANES]
             for t in range(SW_GROUP)], axis=0)
        q4 = jnp.where(upper_lanes if g == 1 else ~upper_lanes, q4, jnp.zeros_like(q4))
        return _dot_nt(q4, kv_rows(kp_ref, kc_ref, blk))

    def exponentials(z, blk, g):
        z = jnp.where(valid_first if blk == 0 else valid_later, z + bias_ref[g], NEG_INF)
        return jnp.exp(z - jnp.max(z, axis=-1, keepdims=True)).astype(_BF16)

    def values(p, blk):
        vv = kv_rows(vp_ref, vc_ref, blk)
        pv = _dot(p, jnp.concatenate([vv, jnp.ones_like(vv)], axis=1))
        return pv[:, :LANES] * (1.0 / pv[:, LANES:])

    items = [(blk, g) for blk in range(SW_SUB) for g in range(SW_KV_HEADS)]
    zs, ps, outs = {}, {}, {}
    for step in range(len(items) + 2):
        if step < len(items):
            zs[step] = logits(*items[step])
        if 0 <= step - 1 < len(items):
            ps[step - 1] = exponentials(zs.pop(step - 1), *items[step - 1])
        if 0 <= step - 2 < len(items):
            blk, g = items[step - 2]
            outs[g] = values(ps.pop(step - 2), blk)
            if g == SW_KV_HEADS - 1:
                y = jnp.where(upper_lanes, outs.pop(1), outs.pop(0)).astype(o_ref.dtype)
                for t in range(SW_GROUP):
                    o_ref[0, blk * BLOCK:(blk + 1) * BLOCK, t * LANES:(t + 1) * LANES] = (
                        y[t * BLOCK:(t + 1) * BLOCK])


def _sw_attn(qkv, table, *, q_col, k_col, v_col):
    b, s, _ = qkv.shape
    qw = SW_HEADS * HEAD_DIM
    tq = SW_SUB * BLOCK
    assert s % tq == 0 and SW_KV_HEADS * HEAD_DIM == LANES
    prev = lambda bi, n: (bi, jnp.maximum(n * SW_SUB - 1, 0))
    return pl.pallas_call(
        _sw_kernel,
        grid=(b, s // tq),
        in_specs=[
            pl.BlockSpec((1, tq, qw), lambda bi, n: (bi, n, q_col)),
            pl.BlockSpec((1, BLOCK, LANES), lambda bi, n: prev(bi, n) + (k_col,)),
            pl.BlockSpec((1, tq, LANES), lambda bi, n: (bi, n, k_col)),
            pl.BlockSpec((1, BLOCK, LANES), lambda bi, n: prev(bi, n) + (v_col,)),
            pl.BlockSpec((1, tq, LANES), lambda bi, n: (bi, n, v_col)),
            pl.BlockSpec((SW_KV_HEADS, SW_GROUP * BLOCK, 2 * BLOCK), lambda bi, n: (0, 0, 0)),
        ],
        out_specs=pl.BlockSpec((1, tq, qw), lambda bi, n: (bi, n, 0)),
        out_shape=jax.ShapeDtypeStruct((b, s, qw), _BF16),
        compiler_params=pltpu.CompilerParams(
            dimension_semantics=("arbitrary", "arbitrary"), vmem_limit_bytes=VMEM_LIMIT),
        name="sw_attn",
    )(qkv, qkv, qkv, qkv, qkv, table)


MIX_CHUNK = V7X_MXU_DIM


def _mix_kernel(x_ref, g_ref, wg_ref, ysb_ref, ysw_ref, wsb_ref, wsw_ref, wo_ref, o_ref,
                h_scr, merged_scr):
    d = x_ref.shape[1]
    tm = x_ref.shape[0] // ROW_SUB
    n_chunks = d // MIX_CHUNK
    rows = lambda s: slice(s * tm, (s + 1) * tm)

    def prologue(s):
        h_scr[rows(s), :] = _rms(x_ref[rows(s), :], g_ref[...]).astype(_BF16)

    def projections(s, c):
        cols = slice(c * MIX_CHUNK, (c + 1) * MIX_CHUNK)
        gate_cols = slice(d + c * MIX_CHUNK, d + (c + 1) * MIX_CHUNK)
        h = h_scr[rows(s), :]
        return (_dot(h, wg_ref[:, cols]), _dot(ysb_ref[rows(s), :], wsb_ref[:, cols]),
                _dot(h, wg_ref[:, gate_cols]), _dot(ysw_ref[rows(s), :], wsw_ref[:, cols]))

    def epilogue(s, y):
        o_ref[rows(s), :] = x_ref[rows(s), :] + y

    prologue(0)
    ahead = projections(0, 0)
    pending = None
    for s in range(ROW_SUB):
        for c in range(n_chunks):
            logit_sb, y_sb, logit_sw, y_sw = ahead
            if c + 1 < n_chunks:
                ahead = projections(s, c + 1)
            elif s + 1 < ROW_SUB:
                ahead = projections(s + 1, 0)
            merged = jax.nn.sigmoid(logit_sb) * y_sb + jax.nn.sigmoid(logit_sw) * y_sw
            merged_scr[rows(s), c * MIX_CHUNK:(c + 1) * MIX_CHUNK] = merged.astype(_BF16)
            if c == 0 and s + 1 < ROW_SUB:
                prologue(s + 1)
            if c == 1 and pending is not None:
                epilogue(*pending)
                pending = None
        pending = (s, _dot(merged_scr[rows(s), :], wo_ref[...]))
    epilogue(*pending)


def _mix_out(x2d, g, w_gate, y_sb, y_sw, w_sb, w_sw, w_out, *, tm):
    n, d = x2d.shape
    assert d % MIX_CHUNK == 0
    step_rows = ROW_SUB * tm
    full = lambda a: pl.BlockSpec(a.shape, lambda i: (0, 0), pipeline_mode=pl.Buffered(1))
    rows = lambda a: pl.BlockSpec((step_rows, a.shape[1]), lambda i: (i, 0))
    return pl.pallas_call(
        _mix_kernel,
        grid=(n // step_rows,),
        in_specs=[rows(x2d), full(g), full(w_gate), rows(y_sb), rows(y_sw),
                  full(w_sb), full(w_sw), full(w_out)],
        out_specs=pl.BlockSpec((step_rows, d), lambda i: (i, 0)),
        out_shape=jax.ShapeDtypeStruct((n, d), _F32),
        scratch_shapes=[pltpu.VMEM((step_rows, d), _BF16), pltpu.VMEM((step_rows, d), _BF16)],
        compiler_params=pltpu.CompilerParams(
            dimension_semantics=("arbitrary",), vmem_limit_bytes=VMEM_LIMIT),
        name="mix_out",
    )(x2d, g, w_gate, y_sb, y_sw, w_sb, w_sw, w_out)


FFN_SUB = 2
HALO = 8


def _ffn_kernel(x_ref, g_ref, wu_ref, cw_ref, cb_ref, wd_ref, gf_ref, o_ref,
                h_scr, act_scr, u_scr, carry_scr, *, steps_per_seq):
    tm = x_ref.shape[0] // FFN_SUB
    d_ff = wd_ref.shape[0]
    nj = d_ff // FF_CHUNK
    rows = lambda s: slice(s * tm, (s + 1) * tm)

    @pl.when(pl.program_id(0) % steps_per_seq == 0)
    def _():
        carry_scr[...] = jnp.zeros_like(carry_scr)

    def prologue(s):
        h_scr[rows(s), :] = _rms(x_ref[rows(s), :], g_ref[...]).astype(_BF16)

    def up_proj(s, j):
        h = h_scr[rows(s), :]
        gate = _dot(h, wu_ref[:, j * FF_CHUNK:(j + 1) * FF_CHUNK])
        value = _dot(h, wu_ref[:, d_ff + j * FF_CHUNK:d_ff + (j + 1) * FF_CHUNK])
        return gate, value

    def conv(slot, j, part, u):
        slabs = []
        for k in range(FF_CHUNK // LANES):
            slab = (slot * 2 + part) * (FF_CHUNK // LANES) + k
            cols = slice(part * d_ff + j * FF_CHUNK + k * LANES,
                         part * d_ff + j * FF_CHUNK + (k + 1) * LANES)
            lanes = slice((part * FF_CHUNK // LANES + k) * LANES,
                          (part * FF_CHUNK // LANES + k + 1) * LANES)
            uk = u[:, k * LANES:(k + 1) * LANES]
            u_scr[slab, HALO:HALO + tm, :] = uk
            u_scr[slab, 0:HALO, :] = carry_scr[j, :, lanes]
            carry_scr[j, :, lanes] = uk[tm - HALO:tm, :]
            y = cb_ref[:, cols] + cw_ref[CONV_WIDTH - 1:CONV_WIDTH, cols] * uk
            for tap in range(CONV_WIDTH - 1):
                back = CONV_WIDTH - 1 - tap
                y = y + cw_ref[tap:tap + 1, cols] * u_scr[slab, pl.ds(HALO - back, tm, stride=1), :]
            slabs.append(y)
        return jnp.concatenate(slabs, axis=1)

    def gated_conv(s, j, gate, value):
        slot = (s * nj + j) % 2
        act = jax.nn.silu(conv(slot, j, 0, gate)) * conv(slot, j, 1, value)
        act_scr[rows(s), j * FF_CHUNK:(j + 1) * FF_CHUNK] = act.astype(_BF16)

    def epilogue(s, y):
        o_ref[rows(s), :] = _rms(x_ref[rows(s), :] + y, gf_ref[...])

    prologue(0)
    ahead = up_proj(0, 0)
    pending = None
    for s in range(FFN_SUB):
        for j in range(nj):
            gate, value = ahead
            if j + 1 < nj:
                ahead = up_proj(s, j + 1)
            elif s + 1 < FFN_SUB:
                ahead = up_proj(s + 1, 0)
            gated_conv(s, j, gate, value)
            if j == 0 and s + 1 < FFN_SUB:
                prologue(s + 1)
            if j == 1 and pending is not None:
                epilogue(*pending)
                pending = None
        pending = (s, _dot(act_scr[rows(s), :], wd_ref[...]))
    epilogue(*pending)


def _conv_ffn(x1, g_ffn, w_up, conv_w, conv_b, w_down, g_final, *, tm, tiles_per_seq):
    n, d = x1.shape
    d_ff = w_down.shape[0]
    nj = d_ff // FF_CHUNK
    assert tiles_per_seq % FFN_SUB == 0
    step_rows = FFN_SUB * tm
    resident = lambda a: pl.BlockSpec(a.shape, lambda i: (0, 0), pipeline_mode=pl.Buffered(1))
    return pl.pallas_call(
        functools.partial(_ffn_kernel, steps_per_seq=tiles_per_seq // FFN_SUB),
        grid=(n // step_rows,),
        in_specs=[
            pl.BlockSpec((step_rows, d), lambda i: (i, 0)),
            resident(g_ffn), resident(w_up), resident(conv_w), resident(conv_b),
            resident(w_down), resident(g_final),
        ],
        out_specs=pl.BlockSpec((step_rows, d), lambda i: (i, 0)),
        out_shape=jax.ShapeDtypeStruct((n, d), _F32),
        scratch_shapes=[
            pltpu.VMEM((step_rows, d), _BF16),
            pltpu.VMEM((step_rows, d_ff), _BF16),
            pltpu.VMEM((2 * 2 * FF_CHUNK // LANES, HALO + tm, LANES), _F32),
            pltpu.VMEM((nj, HALO, 2 * FF_CHUNK), _F32),
        ],
        compiler_params=pltpu.CompilerParams(
            dimension_semantics=("arbitrary",), vmem_limit_bytes=VMEM_LIMIT),
        name="conv_ffn",
    )(x1, g_ffn, w_up, conv_w, conv_b, w_down, g_final)


def kernel(x, g_mix, w_in, w_sb_proj, w_sw_proj, w_out, rel_bias, sinks, g_ffn, w_up,
           conv_w, conv_b, w_down, g_final):
    b, s, d = x.shape
    n = b * s
    d_ff = w_down.shape[0]
    sb_w = SB_HEADS * HEAD_DIM
    sw_qw = SW_HEADS * HEAD_DIM
    sw_kw = SW_KV_HEADS * HEAD_DIM
    assert s % BLOCK == 0 and d_ff % FF_CHUNK == 0
    tm = min(ROW_TILE, s)
    assert s % (max(ROW_SUB, FFN_SUB) * tm) == 0

    qkv_end = 3 * sb_w + sw_qw
    order = np.arange(SW_HEADS).reshape(SW_KV_HEADS, SW_GROUP).T.reshape(-1)
    by_head = lambda a, axis: jnp.take(
        a.reshape(a.shape[:axis] + (SW_HEADS, HEAD_DIM) + a.shape[axis + 1:]), order,
        axis=axis).reshape(a.shape)
    w_in = w_in.astype(_BF16)
    w_qkv = jnp.concatenate(
        [w_in[:, :3 * sb_w], by_head(w_in[:, 3 * sb_w:qkv_end], 1),
         w_in[:, qkv_end:qkv_end + 2 * sw_kw]], axis=1)
    w_gate = w_in[:, qkv_end + 2 * sw_kw:]
    row = lambda a: a.reshape(1, -1).astype(_F32)

    x2d = x.reshape(n, d)
    assert sb_w == IN_CHUNK and sw_qw == IN_CHUNK
    qkv = _in_proj(x2d, row(g_mix), w_qkv, tm=tm, chunk=IN_CHUNK,
                   n_scaled=(0, 3 * sb_w // IN_CHUNK))
    qkv = qkv.reshape(b, s, -1)
    y_sb = _sb_attn(qkv, q_col=0, k_col=sb_w // LANES, v_col=2 * sb_w // LANES)
    y_sw = _sw_attn(qkv, _sw_bias(rel_bias, sinks), q_col=3 * sb_w // sw_qw,
                    k_col=qkv_end // LANES, v_col=qkv_end // LANES + 1)
    x1 = _mix_out(x2d, row(g_mix), w_gate, y_sb.reshape(n, sb_w), y_sw.reshape(n, sw_qw),
                  w_sb_proj.astype(_BF16), by_head(w_sw_proj, 0).astype(_BF16),
                  w_out.astype(_BF16), tm=tm)
    out = _conv_ffn(x1, row(g_ffn), w_up.astype(_BF16), conv_w.astype(_F32), row(conv_b),
                    w_down.astype(_BF16), row(g_final), tm=tm, tiles_per_seq=s // tm)
    return out.reshape(b, s, d)
```

```python
import functools
import math

import jax
import jax.numpy as jnp
import numpy as np
from jax import lax
from jax.experimental import pallas as pl
from jax.experimental.pallas import tpu as pltpu

SB_HEADS = 8
SW_HEADS = 8
SW_KV_HEADS = 2
HEAD_DIM = 64
BLOCK = 128
NUM_BUCKETS = 32
MAX_DISTANCE = 128
CONV_WIDTH = 3
EPS = 1e-6
NEG_INF = -1e30
LANES = 128
V7X_MXU_DIM = 256
V7X_VMEM_BYTES = 64 * 1024 * 1024
VMEM_LIMIT = V7X_VMEM_BYTES * 7 // 8
ROW_TILE = 512
IN_CHUNK = 2 * V7X_MXU_DIM
FF_CHUNK = V7X_MXU_DIM

_BF16 = jnp.bfloat16
_F32 = jnp.float32


def _rms(x, g):
    return x * lax.rsqrt(jnp.mean(x * x, axis=-1, keepdims=True) + EPS) * g


def _dot(a, b):
    return jnp.dot(a, b, preferred_element_type=_F32)


def _dot_nt(a, b):
    return lax.dot_general(a, b, (((1,), (1,)), ((), ())), preferred_element_type=_F32)


ROW_SUB = 2


def _in_proj_kernel(x_ref, g_ref, w_ref, o_ref, h_scr, *, n_scaled, chunk):
    tm = x_ref.shape[0] // ROW_SUB
    rows = lambda s: slice(s * tm, (s + 1) * tm)
    scale = 1.0 / math.sqrt(HEAD_DIM)

    def prologue(s):
        h_scr[rows(s), :] = _rms(x_ref[rows(s), :], g_ref[...]).astype(_BF16)

    prologue(0)
    width = w_ref.shape[1]
    for s in range(ROW_SUB):
        for c, start in enumerate(range(0, width, chunk)):
            cols = slice(start, min(start + chunk, width))
            p = _dot(h_scr[rows(s), :], w_ref[:, cols])
            if c in n_scaled:
                p = p * scale
            o_ref[rows(s), cols] = p.astype(_BF16)
            if c == 0 and s + 1 < ROW_SUB:
                prologue(s + 1)


def _in_proj(x2d, g, w, *, tm, n_scaled, chunk):
    n, d = x2d.shape
    width = w.shape[1]
    step_rows = ROW_SUB * tm
    return pl.pallas_call(
        functools.partial(_in_proj_kernel, n_scaled=n_scaled, chunk=chunk),
        grid=(n // step_rows,),
        in_specs=[
            pl.BlockSpec((step_rows, d), lambda i: (i, 0)),
            pl.BlockSpec((1, d), lambda i: (0, 0), pipeline_mode=pl.Buffered(1)),
            pl.BlockSpec((d, width), lambda i: (0, 0), pipeline_mode=pl.Buffered(1)),
        ],
        out_specs=pl.BlockSpec((step_rows, width), lambda i: (i, 0)),
        out_shape=jax.ShapeDtypeStruct((n, width), _BF16),
        scratch_shapes=[pltpu.VMEM((step_rows, d), _BF16)],
        compiler_params=pltpu.CompilerParams(
            dimension_semantics=("arbitrary",), vmem_limit_bytes=VMEM_LIMIT),
        name="in_proj",
    )(x2d, g, w)


SB_SUB = 64
SB_TOP = 32
SB_MASKED = -1e30
SB_SKEW = 3
SB_OFF = 1e30
LOG2E = 1.4426950408889634
SB_DEAD_LOG2 = 151.0


def _sb_kernel(q_ref, k_ref, v_ref, t_ref, o_ref, rem_scr, acc_scr):
    base = pl.program_id(2) * SB_SUB
    tri = t_ref[...]
    rows2, top2 = 2 * BLOCK, 2 * SB_TOP
    rest = BLOCK - SB_TOP
    lane = lax.broadcasted_iota(jnp.int32, (rows2, LANES), 1)
    row = lax.broadcasted_iota(jnp.int32, (rows2, LANES), 0)
    head1 = ((row >= SB_TOP) & (row < top2)) | (row >= top2 + rest)
    own_lanes = (lane >= HEAD_DIM) == head1
    query = jnp.where(row < SB_TOP, row, jnp.where(row < top2 + rest, row - SB_TOP, row - BLOCK))
    strictly_earlier = lane < query

    def stacked_q(a):
        q = q_ref[0, a * BLOCK:(a + 1) * BLOCK, :]
        q2 = jnp.concatenate([q[:SB_TOP], q[:SB_TOP], q[SB_TOP:], q[SB_TOP:]], axis=0)
        return jnp.where(own_lanes, q2, jnp.zeros_like(q2))

    def key_rows(kb):
        return pl.ds(pl.multiple_of(kb * BLOCK, BLOCK), BLOCK)

    def scores(q2, kb, diagonal):
        z = _dot_nt(q2, k_ref[0, key_rows(kb), :]) * LOG2E
        return jnp.where(strictly_earlier, z, SB_MASKED) if diagonal else z

    def suffix_mass(z):
        sign = jnp.uint32(0x80000000)
        neg_abs = lax.bitcast_convert_type(lax.bitcast_convert_type(z, jnp.uint32) | sign, _F32)
        sp = jnp.maximum(z, 0.0) + jnp.log2(1.0 + jnp.exp2(neg_abs))
        hi = sp.astype(_BF16)
        lo = (sp - hi.astype(_F32)).astype(_BF16)
        return _dot(jnp.concatenate([hi, lo], axis=1), tri)

    def weighted_values(z, cum, rem, kb):
        w = jnp.exp2(z - cum[:, :BLOCK] - rem)
        return _dot(w.astype(_BF16), v_ref[0, key_rows(kb), :])

    def earlier_block(a, t, rem):
        kb = base + a - t
        return jnp.maximum(kb, 0), rem + jnp.where(kb < 0, SB_OFF, 0.0)

    items = [(a, t, slice(0, top2) if t == 2 else slice(0, rows2))
             for a in range(SB_SUB) for t in range(3)]
    q2s, zs, cums = {}, {}, {}
    rems = [jnp.zeros((rows2, BLOCK), _F32) for _ in range(SB_SUB)]
    accs = [jnp.zeros((rows2, LANES), _F32) for _ in range(SB_SUB)]
    for step in range(len(items) + 2 * SB_SKEW):
        if step < len(items):
            a, t, part = items[step]
            if t == 0:
                q2s[a] = stacked_q(a)
            zs[step] = scores(q2s[a][part], jnp.maximum(base + a - t, 0), t == 0)
        i = step - SB_SKEW
        if 0 <= i < len(items):
            cums[i] = suffix_mass(zs[i])
        i = step - 2 * SB_SKEW
        if 0 <= i < len(items):
            a, t, part = items[i]
            kb, rem = earlier_block(a, t, rems[a][part])
            pv = weighted_values(zs.pop(i), cums[i], rem, kb)
            rem = rem + cums.pop(i)[:, BLOCK:]
            acc = accs[a][part] + pv
            if part.stop < rows2:
                rem = jnp.concatenate([rem, rems[a][part.stop:]], axis=0)
                acc = jnp.concatenate([acc, accs[a][part.stop:]], axis=0)
            rems[a], accs[a] = rem, acc
    floor = None
    for a in range(SB_SUB):
        rem_scr[a] = rems[a]
        acc_scr[a] = accs[a]
        floor = rems[a] if floor is None else jnp.minimum(floor, rems[a])

    def cond(carry):
        t, least = carry
        return (base + SB_SUB - 1 - t >= 0) & (least < SB_DEAD_LOG2)

    def body(carry):
        t, _ = carry
        floor = None
        for a in range(SB_SUB):
            q2 = stacked_q(a)
            for part, back in ((slice(top2, rows2), t), (slice(0, top2), t + 1)):
                kb, rem = earlier_block(a, back, rem_scr[a, part, :])
                z = scores(q2[part], kb, False)
                cum = suffix_mass(z)
                acc_scr[a, part, :] += weighted_values(z, cum, rem, kb)
                rem = rem + cum[:, BLOCK:]
                rem_scr[a, part, :] = rem
                least = jnp.min(rem, axis=0, keepdims=True)
                floor = least if floor is None else jnp.minimum(floor, least)
        return t + 1, jnp.min(floor)

    lax.while_loop(cond, body, (jnp.int32(2), jnp.min(floor)))
    head0_lanes = lax.broadcasted_iota(jnp.int32, (BLOCK, LANES), 1) < HEAD_DIM
    for a in range(SB_SUB):
        head0 = jnp.concatenate([acc_scr[a, :SB_TOP, :], acc_scr[a, top2:top2 + rest, :]], axis=0)
        head1 = jnp.concatenate([acc_scr[a, SB_TOP:top2, :], acc_scr[a, top2 + rest:, :]], axis=0)
        o_ref[0, a * BLOCK:(a + 1) * BLOCK, :] = jnp.where(
            head0_lanes, head0, head1).astype(o_ref.dtype)


def _suffix_sum_matrix():
    j = np.arange(2 * BLOCK)[:, None] % BLOCK
    s = np.arange(2 * BLOCK)[None, :]
    return jnp.asarray(((s >= BLOCK) | (j >= s)).astype(np.float32), dtype=_BF16)


def _sb_attn(qkv, *, q_col, k_col, v_col):
    b, s, _ = qkv.shape
    pairs = SB_HEADS * HEAD_DIM // LANES
    tq = SB_SUB * BLOCK
    assert s % tq == 0
    return pl.pallas_call(
        _sb_kernel,
        grid=(b, pairs, s // tq),
        in_specs=[
            pl.BlockSpec((1, tq, LANES), lambda bi, p, qi: (bi, qi, q_col + p)),
            pl.BlockSpec((1, s, LANES), lambda bi, p, qi: (bi, 0, k_col + p)),
            pl.BlockSpec((1, s, LANES), lambda bi, p, qi: (bi, 0, v_col + p)),
            pl.BlockSpec((2 * BLOCK, 2 * BLOCK), lambda bi, p, qi: (0, 0)),
        ],
        out_specs=pl.BlockSpec((1, tq, LANES), lambda bi, p, qi: (bi, qi, p)),
        out_shape=jax.ShapeDtypeStruct((b, s, SB_HEADS * HEAD_DIM), _BF16),
        scratch_shapes=[
            pltpu.VMEM((SB_SUB, 2 * BLOCK, BLOCK), _F32),
            pltpu.VMEM((SB_SUB, 2 * BLOCK, LANES), _F32),
        ],
        compiler_params=pltpu.CompilerParams(
            dimension_semantics=("arbitrary", "arbitrary", "arbitrary"),
            vmem_limit_bytes=VMEM_LIMIT),
        name="sb_attn",
    )(qkv, qkv, qkv, _suffix_sum_matrix())


def _sw_bias_kernel(bucket_ref, rel_ref, sink_ref, o_ref):
    bucket = bucket_ref[...]
    sink_col = lax.broadcasted_iota(jnp.int32, bucket.shape, 1) == 0
    for h in range(SW_HEADS):
        acc = jnp.zeros(bucket.shape, _F32)
        for b in range(NUM_BUCKETS):
            acc = jnp.where(bucket == b, rel_ref[b, h], acc)
        o_ref[h // SW_GROUP, (h % SW_GROUP) * BLOCK:(h % SW_GROUP + 1) * BLOCK, :] = jnp.where(
            sink_col, sink_ref[h], acc)


def _t5_bucket(dist):
    max_exact = NUM_BUCKETS // 2
    d = np.maximum(dist, 1).astype(np.float32)
    large = max_exact + (np.log(d / max_exact) / math.log(MAX_DISTANCE / max_exact)
                         * (NUM_BUCKETS - max_exact)).astype(np.int32)
    large = np.minimum(large, NUM_BUCKETS - 1)
    return np.where(dist < max_exact, dist, large).astype(np.int32)


def _sw_bias(rel_bias, sinks):
    dist = (np.arange(BLOCK)[:, None] + BLOCK) - np.arange(2 * BLOCK)[None, :]
    bucket = jnp.asarray(_t5_bucket(np.maximum(dist, 0)))
    return pl.pallas_call(
        _sw_bias_kernel,
        in_specs=[
            pl.BlockSpec(memory_space=pltpu.VMEM),
            pl.BlockSpec(memory_space=pltpu.SMEM),
            pl.BlockSpec(memory_space=pltpu.SMEM),
        ],
        out_specs=pl.BlockSpec(memory_space=pltpu.VMEM),
        out_shape=jax.ShapeDtypeStruct((SW_KV_HEADS, SW_GROUP * BLOCK, 2 * BLOCK), _F32),
        name="sw_bias",
    )(bucket, rel_bias.astype(_F32), sinks.astype(_F32))


SW_SUB = 32
SW_GROUP = SW_HEADS // SW_KV_HEADS


def _sw_kernel(q_ref, kp_ref, kc_ref, vp_ref, vc_ref, bias_ref, o_ref):
    rows = SW_GROUP * BLOCK
    upper_lanes = lax.broadcasted_iota(jnp.int32, (rows, LANES), 1) >= HEAD_DIM
    sink_key = lax.broadcasted_iota(jnp.int32, (2 * BLOCK, LANES), 0) == 0
    qpos = (lax.broadcasted_iota(jnp.int32, (rows, 2 * BLOCK), 0) & (BLOCK - 1)) + BLOCK
    kpos = lax.broadcasted_iota(jnp.int32, (rows, 2 * BLOCK), 1)
    dist = qpos - kpos
    in_window = (dist >= 0) & (dist < BLOCK)
    first_key = jnp.where(pl.program_id(1) > 0, 0, BLOCK)
    valid_first = (in_window & (kpos >= first_key)) | (kpos == 0)
    valid_later = in_window | (kpos == 0)

    def kv_rows(prev_ref, cur_ref, blk):
        cur = cur_ref[0, blk * BLOCK:(blk + 1) * BLOCK, :]
        prev = prev_ref[0] if blk == 0 else cur_ref[0, (blk - 1) * BLOCK:blk * BLOCK, :]
        kv = jnp.concatenate([prev, cur], axis=0)
        return jnp.where(sink_key, jnp.zeros_like(kv), kv)

    def logits(blk, g):
        q4 = jnp.concatenate(
            [q_ref[0, blk * BLOCK:(blk + 1) * BLOCK, t * LANES:(t + 1) * LANES]
             for t in range(SW_GROUP)], axis=0)
        q4 = jnp.where(upper_lanes if g == 1 else ~upper_lanes, q4, jnp.zeros_like(q4))
        return _dot_nt(q4, kv_rows(kp_ref, kc_ref, blk))

    def exponentials(z, blk, g):
        z = jnp.where(valid_first if blk == 0 else valid_later, z + bias_ref[g], NEG_INF)
        return jnp.exp(z - jnp.max(z, axis=-1, keepdims=True)).astype(_BF16)

    def values(p, blk):
        vv = kv_rows(vp_ref, vc_ref, blk)
        pv = _dot(p, jnp.concatenate([vv, jnp.ones_like(vv)], axis=1))
        return pv[:, :LANES] * (1.0 / pv[:, LANES:])

    items = [(blk, g) for blk in range(SW_SUB) for g in range(SW_KV_HEADS)]
    zs, ps, outs = {}, {}, {}
    for step in range(len(items) + 2):
        if step < len(items):
            zs[step] = logits(*items[step])
        if 0 <= step - 1 < len(items):
            ps[step - 1] = exponentials(zs.pop(step - 1), *items[step - 1])
        if 0 <= step - 2 < len(items):
            blk, g = items[step - 2]
            outs[g] = values(ps.pop(step - 2), blk)
            if g == SW_KV_HEADS - 1:
                y = jnp.where(upper_lanes, outs.pop(1), outs.pop(0)).astype(o_ref.dtype)
                for t in range(SW_GROUP):
                    o_ref[0, blk * BLOCK:(blk + 1) * BLOCK, t * LANES:(t + 1) * LANES] = (
                        y[t * BLOCK:(t + 1) * BLOCK])


def _sw_attn(qkv, table, *, q_col, k_col, v_col):
    b, s, _ = qkv.shape
    qw = SW_HEADS * HEAD_DIM
    tq = SW_SUB * BLOCK
    assert s % tq == 0 and SW_KV_HEADS * HEAD_DIM == LANES
    prev = lambda bi, n: (bi, jnp.maximum(n * SW_SUB - 1, 0))
    return pl.pallas_call(
        _sw_kernel,
        grid=(b, s // tq),
        in_specs=[
            pl.BlockSpec((1, tq, qw), lambda bi, n: (bi, n, q_col)),
            pl.BlockSpec((1, BLOCK, LANES), lambda bi, n: prev(bi, n) + (k_col,)),
            pl.BlockSpec((1, tq, LANES), lambda bi, n: (bi, n, k_col)),
            pl.BlockSpec((1, BLOCK, LANES), lambda bi, n: prev(bi, n) + (v_col,)),
            pl.BlockSpec((1, tq, LANES), lambda bi, n: (bi, n, v_col)),
            pl.BlockSpec((SW_KV_HEADS, SW_GROUP * BLOCK, 2 * BLOCK), lambda bi, n: (0, 0, 0)),
        ],
        out_specs=pl.BlockSpec((1, tq, qw), lambda bi, n: (bi, n, 0)),
        out_shape=jax.ShapeDtypeStruct((b, s, qw), _BF16),
        compiler_params=pltpu.CompilerParams(
            dimension_semantics=("arbitrary", "arbitrary"), vmem_limit_bytes=VMEM_LIMIT),
        name="sw_attn",
    )(qkv, qkv, qkv, qkv, qkv, table)


MIX_CHUNK = V7X_MXU_DIM


def _mix_kernel(x_ref, g_ref, wg_ref, ysb_ref, ysw_ref, wsb_ref, wsw_ref, wo_ref, o_ref,
                h_scr, merged_scr):
    d = x_ref.shape[1]
    tm = x_ref.shape[0] // ROW_SUB
    n_chunks = d // MIX_CHUNK
    rows = lambda s: slice(s * tm, (s + 1) * tm)

    def prologue(s):
        h_scr[rows(s), :] = _rms(x_ref[rows(s), :], g_ref[...]).astype(_BF16)

    def projections(s, c):
        cols = slice(c * MIX_CHUNK, (c + 1) * MIX_CHUNK)
        gate_cols = slice(d + c * MIX_CHUNK, d + (c + 1) * MIX_CHUNK)
        h = h_scr[rows(s), :]
        return (_dot(h, wg_ref[:, cols]), _dot(ysb_ref[rows(s), :], wsb_ref[:, cols]),
                _dot(h, wg_ref[:, gate_cols]), _dot(ysw_ref[rows(s), :], wsw_ref[:, cols]))

    def epilogue(s, y):
        o_ref[rows(s), :] = x_ref[rows(s), :] + y

    prologue(0)
    ahead = projections(0, 0)
    pending = None
    for s in range(ROW_SUB):
        for c in range(n_chunks):
            logit_sb, y_sb, logit_sw, y_sw = ahead
            if c + 1 < n_chunks:
                ahead = projections(s, c + 1)
            elif s + 1 < ROW_SUB:
                ahead = projections(s + 1, 0)
            merged = jax.nn.sigmoid(logit_sb) * y_sb + jax.nn.sigmoid(logit_sw) * y_sw
            merged_scr[rows(s), c * MIX_CHUNK:(c + 1) * MIX_CHUNK] = merged.astype(_BF16)
            if c == 0 and s + 1 < ROW_SUB:
                prologue(s + 1)
            if c == 1 and pending is not None:
                epilogue(*pending)
                pending = None
        pending = (s, _dot(merged_scr[rows(s), :], wo_ref[...]))
    epilogue(*pending)


def _mix_out(x2d, g, w_gate, y_sb, y_sw, w_sb, w_sw, w_out, *, tm):
    n, d = x2d.shape
    assert d % MIX_CHUNK == 0
    step_rows = ROW_SUB * tm
    full = lambda a: pl.BlockSpec(a.shape, lambda i: (0, 0), pipeline_mode=pl.Buffered(1))
    rows = lambda a: pl.BlockSpec((step_rows, a.shape[1]), lambda i: (i, 0))
    return pl.pallas_call(
        _mix_kernel,
        grid=(n // step_rows,),
        in_specs=[rows(x2d), full(g), full(w_gate), rows(y_sb), rows(y_sw),
                  full(w_sb), full(w_sw), full(w_out)],
        out_specs=pl.BlockSpec((step_rows, d), lambda i: (i, 0)),
        out_shape=jax.ShapeDtypeStruct((n, d), _F32),
        scratch_shapes=[pltpu.VMEM((step_rows, d), _BF16), pltpu.VMEM((step_rows, d), _BF16)],
        compiler_params=pltpu.CompilerParams(
            dimension_semantics=("arbitrary",), vmem_limit_bytes=VMEM_LIMIT),
        name="mix_out",
    )(x2d, g, w_gate, y_sb, y_sw, w_sb, w_sw, w_out)


FFN_SUB = 2
HALO = 8


def _ffn_kernel(x_ref, g_ref, wu_ref, cw_ref, cb_ref, wd_ref, gf_ref, o_ref,
                h_scr, act_scr, u_scr, carry_scr, *, steps_per_seq):
    tm = x_ref.shape[0] // FFN_SUB
    d_ff = wd_ref.shape[0]
    nj = d_ff // FF_CHUNK
    rows = lambda s: slice(s * tm, (s + 1) * tm)

    @pl.when(pl.program_id(0) % steps_per_seq == 0)
    def _():
        carry_scr[...] = jnp.zeros_like(carry_scr)

    def prologue(s):
        h_scr[rows(s), :] = _rms(x_ref[rows(s), :], g_ref[...]).astype(_BF16)

    def up_proj(s, j):
        h = h_scr[rows(s), :]
        gate = _dot(h, wu_ref[:, j * FF_CHUNK:(j + 1) * FF_CHUNK])
        value = _dot(h, wu_ref[:, d_ff + j * FF_CHUNK:d_ff + (j + 1) * FF_CHUNK])
        return gate, value

    def conv(slot, j, part, u):
        slabs = []
        for k in range(FF_CHUNK // LANES):
            slab = (slot * 2 + part) * (FF_CHUNK // LANES) + k
            cols = slice(part * d_ff + j * FF_CHUNK + k * LANES,
                         part * d_ff + j * FF_CHUNK + (k + 1) * LANES)
            lanes = slice((part * FF_CHUNK // LANES + k) * LANES,
                          (part * FF_CHUNK // LANES + k + 1) * LANES)
            uk = u[:, k * LANES:(k + 1) * LANES]
            u_scr[slab, HALO:HALO + tm, :] = uk
            u_scr[slab, 0:HALO, :] = carry_scr[j, :, lanes]
            carry_scr[j, :, lanes] = uk[tm - HALO:tm, :]
            y = cb_ref[:, cols] + cw_ref[CONV_WIDTH - 1:CONV_WIDTH, cols] * uk
            for tap in range(CONV_WIDTH - 1):
                back = CONV_WIDTH - 1 - tap
                y = y + cw_ref[tap:tap + 1, cols] * u_scr[slab, pl.ds(HALO - back, tm, stride=1), :]
            slabs.append(y)
        return jnp.concatenate(slabs, axis=1)

    def gated_conv(s, j, gate, value):
        slot = (s * nj + j) % 2
        act = jax.nn.silu(conv(slot, j, 0, gate)) * conv(slot, j, 1, value)
        act_scr[rows(s), j * FF_CHUNK:(j + 1) * FF_CHUNK] = act.astype(_BF16)

    def epilogue(s, y):
        o_ref[rows(s), :] = _rms(x_ref[rows(s), :] + y, gf_ref[...])

    prologue(0)
    ahead = up_proj(0, 0)
    pending = None
    for s in range(FFN_SUB):
        for j in range(nj):
            gate, value = ahead
            if j + 1 < nj:
                ahead = up_proj(s, j + 1)
            elif s + 1 < FFN_SUB:
                ahead = up_proj(s + 1, 0)
            gated_conv(s, j, gate, value)
            if j == 0 and s + 1 < FFN_SUB:
                prologue(s + 1)
            if j == 1 and pending is not None:
                epilogue(*pending)
                pending = None
        pending = (s, _dot(act_scr[rows(s), :], wd_ref[...]))
    epilogue(*pending)


def _conv_ffn(x1, g_ffn, w_up, conv_w, conv_b, w_down, g_final, *, tm, tiles_per_seq):
    n, d = x1.shape
    d_ff = w_down.shape[0]
    nj = d_ff // FF_CHUNK
    assert tiles_per_seq % FFN_SUB == 0
    step_rows = FFN_SUB * tm
    resident = lambda a: pl.BlockSpec(a.shape, lambda i: (0, 0), pipeline_mode=pl.Buffered(1))
    return pl.pallas_call(
        functools.partial(_ffn_kernel, steps_per_seq=tiles_per_seq // FFN_SUB),
        grid=(n // step_rows,),
        in_specs=[
            pl.BlockSpec((step_rows, d), lambda i: (i, 0)),
            resident(g_ffn), resident(w_up), resident(conv_w), resident(conv_b),
            resident(w_down), resident(g_final),
        ],
        out_specs=pl.BlockSpec((step_rows, d), lambda i: (i, 0)),
        out_shape=jax.ShapeDtypeStruct((n, d), _F32),
        scratch_shapes=[
            pltpu.VMEM((step_rows, d), _BF16),
            pltpu.VMEM((step_rows, d_ff), _BF16),
            pltpu.VMEM((2 * 2 * FF_CHUNK // LANES, HALO + tm, LANES), _F32),
            pltpu.VMEM((nj, HALO, 2 * FF_CHUNK), _F32),
        ],
        compiler_params=pltpu.CompilerParams(
            dimension_semantics=("arbitrary",), vmem_limit_bytes=VMEM_LIMIT),
        name="conv_ffn",
    )(x1, g_ffn, w_up, conv_w, conv_b, w_down, g_final)


def kernel(x, g_mix, w_in, w_sb_proj, w_sw_proj, w_out, rel_bias, sinks, g_ffn, w_up,
           conv_w, conv_b, w_down, g_final):
    b, s, d = x.shape
    n = b * s
    d_ff = w_down.shape[0]
    sb_w = SB_HEADS * HEAD_DIM
    sw_qw = SW_HEADS * HEAD_DIM
    sw_kw = SW_KV_HEADS * HEAD_DIM
    assert s % BLOCK == 0 and d_ff % FF_CHUNK == 0
    tm = min(ROW_TILE, s)
    assert s % (max(ROW_SUB, FFN_SUB) * tm) == 0

    qkv_end = 3 * sb_w + sw_qw
    order = np.arange(SW_HEADS).reshape(SW_KV_HEADS, SW_GROUP).T.reshape(-1)
    by_head = lambda a, axis: jnp.take(
        a.reshape(a.shape[:axis] + (SW_HEADS, HEAD_DIM) + a.shape[axis + 1:]), order,
        axis=axis).reshape(a.shape)
    w_in = w_in.astype(_BF16)
    w_qkv = jnp.concatenate(
        [w_in[:, :3 * sb_w], by_head(w_in[:, 3 * sb_w:qkv_end], 1),
         w_in[:, qkv_end:qkv_end + 2 * sw_kw]], axis=1)
    w_gate = w_in[:, qkv_end + 2 * sw_kw:]
    row = lambda a: a.reshape(1, -1).astype(_F32)

    x2d = x.reshape(n, d)
    assert sb_w == IN_CHUNK and sw_qw == IN_CHUNK
    qkv = _in_proj(x2d, row(g_mix), w_qkv, tm=tm, chunk=IN_CHUNK,
                   n_scaled=(0, 3 * sb_w // IN_CHUNK))
    qkv = qkv.reshape(b, s, -1)
    y_sb = _sb_attn(qkv, q_col=0, k_col=sb_w // LANES, v_col=2 * sb_w // LANES)
    y_sw = _sw_attn(qkv, _sw_bias(rel_bias, sinks), q_col=3 * sb_w // sw_qw,
                    k_col=qkv_end // LANES, v_col=qkv_end // LANES + 1)
    x1 = _mix_out(x2d, row(g_mix), w_gate, y_sb.reshape(n, sb_w), y_sw.reshape(n, sw_qw),
                  w_sb_proj.astype(_BF16), by_head(w_sw_proj, 0).astype(_BF16),
                  w_out.astype(_BF16), tm=tm)
    out = _conv_ffn(x1, row(g_ffn), w_up.astype(_BF16), conv_w.astype(_F32), row(conv_b),
                    w_down.astype(_BF16), row(g_final), tm=tm, tiles_per_seq=s // tm)
    return out.reshape(b, s, d)
```

```python
import functools
import math

import jax
import jax.numpy as jnp
import numpy as np
from jax import lax
from jax.experimental import pallas as pl
from jax.experimental.pallas import tpu as pltpu

SB_HEADS = 8
SW_HEADS = 8
SW_KV_HEADS = 2
HEAD_DIM = 64
BLOCK = 128
NUM_BUCKETS = 32
MAX_DISTANCE = 128
CONV_WIDTH = 3
EPS = 1e-6
NEG_INF = -1e30
LANES = 128
V7X_MXU_DIM = 256
V7X_VMEM_BYTES = 64 * 1024 * 1024
VMEM_LIMIT = V7X_VMEM_BYTES * 7 // 8
ROW_TILE = 512
IN_CHUNK = 2 * V7X_MXU_DIM
FF_CHUNK = V7X_MXU_DIM

_BF16 = jnp.bfloat16
_F32 = jnp.float32


def _rms(x, g):
    return x * lax.rsqrt(jnp.mean(x * x, axis=-1, keepdims=True) + EPS) * g


def _dot(a, b):
    return jnp.dot(a, b, preferred_element_type=_F32)


def _dot_nt(a, b):
    return lax.dot_general(a, b, (((1,), (1,)), ((), ())), preferred_element_type=_F32)


ROW_SUB = 2


def _in_proj_kernel(x_ref, g_ref, w_ref, o_ref, h_scr, *, n_scaled, chunk):
    tm = x_ref.shape[0] // ROW_SUB
    rows = lambda s: slice(s * tm, (s + 1) * tm)
    scale = 1.0 / math.sqrt(HEAD_DIM)

    def prologue(s):
        h_scr[rows(s), :] = _rms(x_ref[rows(s), :], g_ref[...]).astype(_BF16)

    prologue(0)
    width = w_ref.shape[1]
    for s in range(ROW_SUB):
        for c, start in enumerate(range(0, width, chunk)):
            cols = slice(start, min(start + chunk, width))
            p = _dot(h_scr[rows(s), :], w_ref[:, cols])
            if c in n_scaled:
                p = p * scale
            o_ref[rows(s), cols] = p.astype(_BF16)
            if c == 0 and s + 1 < ROW_SUB:
                prologue(s + 1)


def _in_proj(x2d, g, w, *, tm, n_scaled, chunk):
    n, d = x2d.shape
    width = w.shape[1]
    step_rows = ROW_SUB * tm
    return pl.pallas_call(
        functools.partial(_in_proj_kernel, n_scaled=n_scaled, chunk=chunk),
        grid=(n // step_rows,),
        in_specs=[
            pl.BlockSpec((step_rows, d), lambda i: (i, 0)),
            pl.BlockSpec((1, d), lambda i: (0, 0), pipeline_mode=pl.Buffered(1)),
            pl.BlockSpec((d, width), lambda i: (0, 0), pipeline_mode=pl.Buffered(1)),
        ],
        out_specs=pl.BlockSpec((step_rows, width), lambda i: (i, 0)),
        out_shape=jax.ShapeDtypeStruct((n, width), _BF16),
        scratch_shapes=[pltpu.VMEM((step_rows, d), _BF16)],
        compiler_params=pltpu.CompilerParams(
            dimension_semantics=("arbitrary",), vmem_limit_bytes=VMEM_LIMIT),
        name="in_proj",
    )(x2d, g, w)


SB_SUB = 32
SB_TOP = 32
SB_MASKED = -1e30
SB_SKEW = 3
SB_OFF = 1e30
LOG2E = 1.4426950408889634
SB_DEAD_LOG2 = 151.0


def _sb_kernel(q_ref, k_ref, v_ref, t_ref, o_ref, rem_scr, acc_scr):
    base = pl.program_id(2) * SB_SUB
    tri = t_ref[...]
    rows2, top2 = 2 * BLOCK, 2 * SB_TOP
    rest = BLOCK - SB_TOP
    lane = lax.broadcasted_iota(jnp.int32, (rows2, LANES), 1)
    row = lax.broadcasted_iota(jnp.int32, (rows2, LANES), 0)
    head1 = ((row >= SB_TOP) & (row < top2)) | (row >= top2 + rest)
    own_lanes = (lane >= HEAD_DIM) == head1
    query = jnp.where(row < SB_TOP, row, jnp.where(row < top2 + rest, row - SB_TOP, row - BLOCK))
    strictly_earlier = lane < query

    def stacked_q(a):
        q = q_ref[0, a * BLOCK:(a + 1) * BLOCK, :]
        q2 = jnp.concatenate([q[:SB_TOP], q[:SB_TOP], q[SB_TOP:], q[SB_TOP:]], axis=0)
        return jnp.where(own_lanes, q2, jnp.zeros_like(q2))

    def key_rows(kb):
        return pl.ds(pl.multiple_of(kb * BLOCK, BLOCK), BLOCK)

    def scores(q2, kb, diagonal):
        z = _dot_nt(q2, k_ref[0, key_rows(kb), :]) * LOG2E
        return jnp.where(strictly_earlier, z, SB_MASKED) if diagonal else z

    def suffix_mass(z):
        sign = jnp.uint32(0x80000000)
        neg_abs = lax.bitcast_convert_type(lax.bitcast_convert_type(z, jnp.uint32) | sign, _F32)
        sp = jnp.maximum(z, 0.0) + jnp.log2(1.0 + jnp.exp2(neg_abs))
        hi = sp.astype(_BF16)
        lo = (sp - hi.astype(_F32)).astype(_BF16)
        return _dot(jnp.concatenate([hi, lo], axis=1), tri)

    def weighted_values(z, cum, rem, kb):
        arg = z - cum[:, :BLOCK]
        w = jnp.exp2(arg if rem is None else arg - rem)
        return _dot(w.astype(_BF16), v_ref[0, key_rows(kb), :])

    def earlier_block(a, t, rem):
        kb = base + a - t
        return jnp.maximum(kb, 0), rem + jnp.where(kb < 0, SB_OFF, 0.0)

    items = [(a, t, slice(0, top2) if t == 2 else slice(0, rows2))
             for a in range(SB_SUB) for t in range(3)]
    q2s, zs, cums = {}, {}, {}
    rems, accs = [None] * SB_SUB, [None] * SB_SUB
    for step in range(len(items) + 2 * SB_SKEW):
        if step < len(items):
            a, t, part = items[step]
            if t == 0:
                q2s[a] = stacked_q(a)
            zs[step] = scores(q2s[a][part], jnp.maximum(base + a - t, 0), t == 0)
        i = step - SB_SKEW
        if 0 <= i < len(items):
            cums[i] = suffix_mass(zs[i])
        i = step - 2 * SB_SKEW
        if 0 <= i < len(items):
            a, t, part = items[i]
            if t == 0:
                kb, rem = base + a, None
            elif a >= t:
                kb, rem = base + a - t, rems[a][part]
            else:
                kb, rem = earlier_block(a, t, rems[a][part])
            acc = weighted_values(zs.pop(i), cums[i], rem, kb)
            total = cums.pop(i)[:, BLOCK:]
            if t > 0:
                rem, acc = rem + total, accs[a][part] + acc
            else:
                rem = total
            if part.stop < rows2:
                rem = jnp.concatenate([rem, rems[a][part.stop:]], axis=0)
                acc = jnp.concatenate([acc, accs[a][part.stop:]], axis=0)
            rems[a], accs[a] = rem, acc
    floor = None
    for a in range(SB_SUB):
        rem_scr[a] = rems[a]
        acc_scr[a] = accs[a]
        floor = rems[a] if floor is None else jnp.minimum(floor, rems[a])

    def cond(carry):
        t, least = carry
        return (base + SB_SUB - 1 - t >= 0) & (least < SB_DEAD_LOG2)

    def body(carry):
        t, _ = carry
        floor = None
        for a in range(SB_SUB):
            q2 = stacked_q(a)
            for part, back in ((slice(top2, rows2), t), (slice(0, top2), t + 1)):
                kb, rem = earlier_block(a, back, rem_scr[a, part, :])
                z = scores(q2[part], kb, False)
                cum = suffix_mass(z)
                acc_scr[a, part, :] += weighted_values(z, cum, rem, kb)
                rem = rem + cum[:, BLOCK:]
                rem_scr[a, part, :] = rem
                least = jnp.min(rem, axis=0, keepdims=True)
                floor = least if floor is None else jnp.minimum(floor, least)
        return t + 1, jnp.min(floor)

    lax.while_loop(cond, body, (jnp.int32(2), jnp.min(floor)))
    head0_lanes = lax.broadcasted_iota(jnp.int32, (BLOCK, LANES), 1) < HEAD_DIM
    for a in range(SB_SUB):
        head0 = jnp.concatenate([acc_scr[a, :SB_TOP, :], acc_scr[a, top2:top2 + rest, :]], axis=0)
        head1 = jnp.concatenate([acc_scr[a, SB_TOP:top2, :], acc_scr[a, top2 + rest:, :]], axis=0)
        o_ref[0, a * BLOCK:(a + 1) * BLOCK, :] = jnp.where(
            head0_lanes, head0, head1).astype(o_ref.dtype)


def _suffix_sum_matrix():
    j = np.arange(2 * BLOCK)[:, None] % BLOCK
    s = np.arange(2 * BLOCK)[None, :]
    return jnp.asarray(((s >= BLOCK) | (j >= s)).astype(np.float32), dtype=_BF16)


def _sb_attn(qkv, *, q_col, k_col, v_col):
    b, s, _ = qkv.shape
    pairs = SB_HEADS * HEAD_DIM // LANES
    tq = SB_SUB * BLOCK
    assert s % tq == 0
    return pl.pallas_call(
        _sb_kernel,
        grid=(b, pairs, s // tq),
        in_specs=[
            pl.BlockSpec((1, tq, LANES), lambda bi, p, qi: (bi, qi, q_col + p)),
            pl.BlockSpec((1, s, LANES), lambda bi, p, qi: (bi, 0, k_col + p)),
            pl.BlockSpec((1, s, LANES), lambda bi, p, qi: (bi, 0, v_col + p)),
            pl.BlockSpec((2 * BLOCK, 2 * BLOCK), lambda bi, p, qi: (0, 0)),
        ],
        out_specs=pl.BlockSpec((1, tq, LANES), lambda bi, p, qi: (bi, qi, p)),
        out_shape=jax.ShapeDtypeStruct((b, s, SB_HEADS * HEAD_DIM), _BF16),
        scratch_shapes=[
            pltpu.VMEM((SB_SUB, 2 * BLOCK, BLOCK), _F32),
            pltpu.VMEM((SB_SUB, 2 * BLOCK, LANES), _F32),
        ],
        compiler_params=pltpu.CompilerParams(
            dimension_semantics=("arbitrary", "arbitrary", "arbitrary"),
            vmem_limit_bytes=VMEM_LIMIT),
        name="sb_attn",
    )(qkv, qkv, qkv, _suffix_sum_matrix())


def _sw_bias_kernel(bucket_ref, rel_ref, sink_ref, o_ref):
    bucket = bucket_ref[...]
    sink_col = lax.broadcasted_iota(jnp.int32, bucket.shape, 1) == 0
    for h in range(SW_HEADS):
        acc = jnp.zeros(bucket.shape, _F32)
        for b in range(NUM_BUCKETS):
            acc = jnp.where(bucket == b, rel_ref[b, h], acc)
        o_ref[h // SW_GROUP, (h % SW_GROUP) * BLOCK:(h % SW_GROUP + 1) * BLOCK, :] = jnp.where(
            sink_col, sink_ref[h], acc)


def _t5_bucket(dist):
    max_exact = NUM_BUCKETS // 2
    d = np.maximum(dist, 1).astype(np.float32)
    large = max_exact + (np.log(d / max_exact) / math.log(MAX_DISTANCE / max_exact)
                         * (NUM_BUCKETS - max_exact)).astype(np.int32)
    large = np.minimum(large, NUM_BUCKETS - 1)
    return np.where(dist < max_exact, dist, large).astype(np.int32)


def _sw_bias(rel_bias, sinks):
    dist = (np.arange(BLOCK)[:, None] + BLOCK) - np.arange(2 * BLOCK)[None, :]
    bucket = jnp.asarray(_t5_bucket(np.maximum(dist, 0)))
    return pl.pallas_call(
        _sw_bias_kernel,
        in_specs=[
            pl.BlockSpec(memory_space=pltpu.VMEM),
            pl.BlockSpec(memory_space=pltpu.SMEM),
            pl.BlockSpec(memory_space=pltpu.SMEM),
        ],
        out_specs=pl.BlockSpec(memory_space=pltpu.VMEM),
        out_shape=jax.ShapeDtypeStruct((SW_KV_HEADS, SW_GROUP * BLOCK, 2 * BLOCK), _F32),
        name="sw_bias",
    )(bucket, rel_bias.astype(_F32), sinks.astype(_F32))


SW_SUB = 32
SW_GROUP = SW_HEADS // SW_KV_HEADS


def _sw_kernel(q_ref, kp_ref, kc_ref, vp_ref, vc_ref, bias_ref, o_ref):
    rows = SW_GROUP * BLOCK
    upper_lanes = lax.broadcasted_iota(jnp.int32, (rows, LANES), 1) >= HEAD_DIM
    sink_key = lax.broadcasted_iota(jnp.int32, (2 * BLOCK, LANES), 0) == 0
    qpos = (lax.broadcasted_iota(jnp.int32, (rows, 2 * BLOCK), 0) & (BLOCK - 1)) + BLOCK
    kpos = lax.broadcasted_iota(jnp.int32, (rows, 2 * BLOCK), 1)
    dist = qpos - kpos
    in_window = (dist >= 0) & (dist < BLOCK)
    first_key = jnp.where(pl.program_id(1) > 0, 0, BLOCK)
    valid_first = (in_window & (kpos >= first_key)) | (kpos == 0)
    valid_later = in_window | (kpos == 0)

    def kv_rows(prev_ref, cur_ref, blk):
        cur = cur_ref[0, blk * BLOCK:(blk + 1) * BLOCK, :]
        prev = prev_ref[0] if blk == 0 else cur_ref[0, (blk - 1) * BLOCK:blk * BLOCK, :]
        kv = jnp.concatenate([prev, cur], axis=0)
        return jnp.where(sink_key, jnp.zeros_like(kv), kv)

    def logits(blk, g):
        q4 = jnp.concatenate(
            [q_ref[0, blk * BLOCK:(blk + 1) * BLOCK, t * LANES:(t + 1) * LANES]
             for t in range(SW_GROUP)], axis=0)
        q4 = jnp.where(upper_lanes if g == 1 else ~upper_lanes, q4, jnp.zeros_like(q4))
        return _dot_nt(q4, kv_rows(kp_ref, kc_ref, blk))

    def exponentials(z, blk, g):
        z = jnp.where(valid_first if blk == 0 else valid_later, z + bias_ref[g], NEG_INF)
        return jnp.exp(z - jnp.max(z, axis=-1, keepdims=True)).astype(_BF16)

    def values(p, blk):
        vv = kv_rows(vp_ref, vc_ref, blk)
        pv = _dot(p, jnp.concatenate([vv, jnp.ones_like(vv)], axis=1))
        return pv[:, :LANES] * (1.0 / pv[:, LANES:])

    items = [(blk, g) for blk in range(SW_SUB) for g in range(SW_KV_HEADS)]
    zs, ps, outs = {}, {}, {}
    for step in range(len(items) + 2):
        if step < len(items):
            zs[step] = logits(*items[step])
        if 0 <= step - 1 < len(items):
            ps[step - 1] = exponentials(zs.pop(step - 1), *items[step - 1])
        if 0 <= step - 2 < len(items):
            blk, g = items[step - 2]
            outs[g] = values(ps.pop(step - 2), blk)
            if g == SW_KV_HEADS - 1:
                y = jnp.where(upper_lanes, outs.pop(1), outs.pop(0)).astype(o_ref.dtype)
                for t in range(SW_GROUP):
                    o_ref[0, blk * BLOCK:(blk + 1) * BLOCK, t * LANES:(t + 1) * LANES] = (
                        y[t * BLOCK:(t + 1) * BLOCK])


def _sw_attn(qkv, table, *, q_col, k_col, v_col):
    b, s, _ = qkv.shape
    qw = SW_HEADS * HEAD_DIM
    tq = SW_SUB * BLOCK
    assert s % tq == 0 and SW_KV_HEADS * HEAD_DIM == LANES
    prev = lambda bi, n: (bi, jnp.maximum(n * SW_SUB - 1, 0))
    return pl.pallas_call(
        _sw_kernel,
        grid=(b, s // tq),
        in_specs=[
            pl.BlockSpec((1, tq, qw), lambda bi, n: (bi, n, q_col)),
            pl.BlockSpec((1, BLOCK, LANES), lambda bi, n: prev(bi, n) + (k_col,)),
            pl.BlockSpec((1, tq, LANES), lambda bi, n: (bi, n, k_col)),
            pl.BlockSpec((1, BLOCK, LANES), lambda bi, n: prev(bi, n) + (v_col,)),
            pl.BlockSpec((1, tq, LANES), lambda bi, n: (bi, n, v_col)),
            pl.BlockSpec((SW_KV_HEADS, SW_GROUP * BLOCK, 2 * BLOCK), lambda bi, n: (0, 0, 0)),
        ],
        out_specs=pl.BlockSpec((1, tq, qw), lambda bi, n: (bi, n, 0)),
        out_shape=jax.ShapeDtypeStruct((b, s, qw), _BF16),
        compiler_params=pltpu.CompilerParams(
            dimension_semantics=("arbitrary", "arbitrary"), vmem_limit_bytes=VMEM_LIMIT),
        name="sw_attn",
    )(qkv, qkv, qkv, qkv, qkv, table)


MIX_CHUNK = V7X_MXU_DIM


def _mix_kernel(x_ref, g_ref, wg_ref, ysb_ref, ysw_ref, wsb_ref, wsw_ref, wo_ref, o_ref,
                h_scr, merged_scr):
    d = x_ref.shape[1]
    tm = x_ref.shape[0] // ROW_SUB
    n_chunks = d // MIX_CHUNK
    rows = lambda s: slice(s * tm, (s + 1) * tm)

    def prologue(s):
        h_scr[rows(s), :] = _rms(x_ref[rows(s), :], g_ref[...]).astype(_BF16)

    def projections(s, c):
        cols = slice(c * MIX_CHUNK, (c + 1) * MIX_CHUNK)
        gate_cols = slice(d + c * MIX_CHUNK, d + (c + 1) * MIX_CHUNK)
        h = h_scr[rows(s), :]
        return (_dot(h, wg_ref[:, cols]), _dot(ysb_ref[rows(s), :], wsb_ref[:, cols]),
                _dot(h, wg_ref[:, gate_cols]), _dot(ysw_ref[rows(s), :], wsw_ref[:, cols]))

    def epilogue(s, y):
        o_ref[rows(s), :] = x_ref[rows(s), :] + y

    prologue(0)
    ahead = projections(0, 0)
    pending = None
    for s in range(ROW_SUB):
        for c in range(n_chunks):
            logit_sb, y_sb, logit_sw, y_sw = ahead
            if c + 1 < n_chunks:
                ahead = projections(s, c + 1)
            elif s + 1 < ROW_SUB:
                ahead = projections(s + 1, 0)
            merged = jax.nn.sigmoid(logit_sb) * y_sb + jax.nn.sigmoid(logit_sw) * y_sw
            merged_scr[rows(s), c * MIX_CHUNK:(c + 1) * MIX_CHUNK] = merged.astype(_BF16)
            if c == 0 and s + 1 < ROW_SUB:
                prologue(s + 1)
            if c == 1 and pending is not None:
                epilogue(*pending)
                pending = None
        pending = (s, _dot(merged_scr[rows(s), :], wo_ref[...]))
    epilogue(*pending)


def _mix_out(x2d, g, w_gate, y_sb, y_sw, w_sb, w_sw, w_out, *, tm):
    n, d = x2d.shape
    assert d % MIX_CHUNK == 0
    step_rows = ROW_SUB * tm
    full = lambda a: pl.BlockSpec(a.shape, lambda i: (0, 0), pipeline_mode=pl.Buffered(1))
    rows = lambda a: pl.BlockSpec((step_rows, a.shape[1]), lambda i: (i, 0))
    return pl.pallas_call(
        _mix_kernel,
        grid=(n // step_rows,),
        in_specs=[rows(x2d), full(g), full(w_gate), rows(y_sb), rows(y_sw),
                  full(w_sb), full(w_sw), full(w_out)],
        out_specs=pl.BlockSpec((step_rows, d), lambda i: (i, 0)),
        out_shape=jax.ShapeDtypeStruct((n, d), _F32),
        scratch_shapes=[pltpu.VMEM((step_rows, d), _BF16), pltpu.VMEM((step_rows, d), _BF16)],
        compiler_params=pltpu.CompilerParams(
            dimension_semantics=("arbitrary",), vmem_limit_bytes=VMEM_LIMIT),
        name="mix_out",
    )(x2d, g, w_gate, y_sb, y_sw, w_sb, w_sw, w_out)


FFN_SUB = 2
HALO = 8


def _ffn_kernel(x_ref, g_ref, wu_ref, cw_ref, cb_ref, wd_ref, gf_ref, o_ref,
                h_scr, act_scr, u_scr, carry_scr, *, steps_per_seq):
    tm = x_ref.shape[0] // FFN_SUB
    d_ff = wd_ref.shape[0]
    nj = d_ff // FF_CHUNK
    rows = lambda s: slice(s * tm, (s + 1) * tm)

    @pl.when(pl.program_id(0) % steps_per_seq == 0)
    def _():
        carry_scr[...] = jnp.zeros_like(carry_scr)

    def prologue(s):
        h_scr[rows(s), :] = _rms(x_ref[rows(s), :], g_ref[...]).astype(_BF16)

    def up_proj(s, j):
        h = h_scr[rows(s), :]
        gate = _dot(h, wu_ref[:, j * FF_CHUNK:(j + 1) * FF_CHUNK])
        value = _dot(h, wu_ref[:, d_ff + j * FF_CHUNK:d_ff + (j + 1) * FF_CHUNK])
        return gate, value

    def conv(slot, j, part, u):
        slabs = []
        for k in range(FF_CHUNK // LANES):
            slab = (slot * 2 + part) * (FF_CHUNK // LANES) + k
            cols = slice(part * d_ff + j * FF_CHUNK + k * LANES,
                         part * d_ff + j * FF_CHUNK + (k + 1) * LANES)
            lanes = slice((part * FF_CHUNK // LANES + k) * LANES,
                          (part * FF_CHUNK // LANES + k + 1) * LANES)
            uk = u[:, k * LANES:(k + 1) * LANES]
            u_scr[slab, HALO:HALO + tm, :] = uk
            u_scr[slab, 0:HALO, :] = carry_scr[j, :, lanes]
            carry_scr[j, :, lanes] = uk[tm - HALO:tm, :]
            y = cb_ref[:, cols] + cw_ref[CONV_WIDTH - 1:CONV_WIDTH, cols] * uk
            for tap in range(CONV_WIDTH - 1):
                back = CONV_WIDTH - 1 - tap
                y = y + cw_ref[tap:tap + 1, cols] * u_scr[slab, pl.ds(HALO - back, tm, stride=1), :]
            slabs.append(y)
        return jnp.concatenate(slabs, axis=1)

    def gated_conv(s, j, gate, value):
        slot = (s * nj + j) % 2
        act = jax.nn.silu(conv(slot, j, 0, gate)) * conv(slot, j, 1, value)
        act_scr[rows(s), j * FF_CHUNK:(j + 1) * FF_CHUNK] = act.astype(_BF16)

    def epilogue(s, y):
        o_ref[rows(s), :] = _rms(x_ref[rows(s), :] + y, gf_ref[...])

    prologue(0)
    ahead = up_proj(0, 0)
    pending = None
    for s in range(FFN_SUB):
        for j in range(nj):
            gate, value = ahead
            if j + 1 < nj:
                ahead = up_proj(s, j + 1)
            elif s + 1 < FFN_SUB:
                ahead = up_proj(s + 1, 0)
            gated_conv(s, j, gate, value)
            if j == 0 and s + 1 < FFN_SUB:
                prologue(s + 1)
            if j == 1 and pending is not None:
                epilogue(*pending)
                pending = None
        pending = (s, _dot(act_scr[rows(s), :], wd_ref[...]))
    epilogue(*pending)


def _conv_ffn(x1, g_ffn, w_up, conv_w, conv_b, w_down, g_final, *, tm, tiles_per_seq):
    n, d = x1.shape
    d_ff = w_down.shape[0]
    nj = d_ff // FF_CHUNK
    assert tiles_per_seq % FFN_SUB == 0
    step_rows = FFN_SUB * tm
    resident = lambda a: pl.BlockSpec(a.shape, lambda i: (0, 0), pipeline_mode=pl.Buffered(1))
    return pl.pallas_call(
        functools.partial(_ffn_kernel, steps_per_seq=tiles_per_seq // FFN_SUB),
        grid=(n // step_rows,),
        in_specs=[
            pl.BlockSpec((step_rows, d), lambda i: (i, 0)),
            resident(g_ffn), resident(w_up), resident(conv_w), resident(conv_b),
            resident(w_down), resident(g_final),
        ],
        out_specs=pl.BlockSpec((step_rows, d), lambda i: (i, 0)),
        out_shape=jax.ShapeDtypeStruct((n, d), _F32),
        scratch_shapes=[
            pltpu.VMEM((step_rows, d), _BF16),
            pltpu.VMEM((step_rows, d_ff), _BF16),
            pltpu.VMEM((2 * 2 * FF_CHUNK // LANES, HALO + tm, LANES), _F32),
            pltpu.VMEM((nj, HALO, 2 * FF_CHUNK), _F32),
        ],
        compiler_params=pltpu.CompilerParams(
            dimension_semantics=("arbitrary",), vmem_limit_bytes=VMEM_LIMIT),
        name="conv_ffn",
    )(x1, g_ffn, w_up, conv_w, conv_b, w_down, g_final)


def kernel(x, g_mix, w_in, w_sb_proj, w_sw_proj, w_out, rel_bias, sinks, g_ffn, w_up,
           conv_w, conv_b, w_down, g_final):
    b, s, d = x.shape
    n = b * s
    d_ff = w_down.shape[0]
    sb_w = SB_HEADS * HEAD_DIM
    sw_qw = SW_HEADS * HEAD_DIM
    sw_kw = SW_KV_HEADS * HEAD_DIM
    assert s % BLOCK == 0 and d_ff % FF_CHUNK == 0
    tm = min(ROW_TILE, s)
    assert s % (max(ROW_SUB, FFN_SUB) * tm) == 0

    qkv_end = 3 * sb_w + sw_qw
    order = np.arange(SW_HEADS).reshape(SW_KV_HEADS, SW_GROUP).T.reshape(-1)
    by_head = lambda a, axis: jnp.take(
        a.reshape(a.shape[:axis] + (SW_HEADS, HEAD_DIM) + a.shape[axis + 1:]), order,
        axis=axis).reshape(a.shape)
    w_in = w_in.astype(_BF16)
    w_qkv = jnp.concatenate(
        [w_in[:, :3 * sb_w], by_head(w_in[:, 3 * sb_w:qkv_end], 1),
         w_in[:, qkv_end:qkv_end + 2 * sw_kw]], axis=1)
    w_gate = w_in[:, qkv_end + 2 * sw_kw:]
    row = lambda a: a.reshape(1, -1).astype(_F32)

    x2d = x.reshape(n, d)
    assert sb_w == IN_CHUNK and sw_qw == IN_CHUNK
    qkv = _in_proj(x2d, row(g_mix), w_qkv, tm=tm, chunk=IN_CHUNK,
                   n_scaled=(0, 3 * sb_w // IN_CHUNK))
    qkv = qkv.reshape(b, s, -1)
    y_sb = _sb_attn(qkv, q_col=0, k_col=sb_w // LANES, v_col=2 * sb_w // LANES)
    y_sw = _sw_attn(qkv, _sw_bias(rel_bias, sinks), q_col=3 * sb_w // sw_qw,
                    k_col=qkv_end // LANES, v_col=qkv_end // LANES + 1)
    x1 = _mix_out(x2d, row(g_mix), w_gate, y_sb.reshape(n, sb_w), y_sw.reshape(n, sw_qw),
                  w_sb_proj.astype(_BF16), by_head(w_sw_proj, 0).astype(_BF16),
                  w_out.astype(_BF16), tm=tm)
    out = _conv_ffn(x1, row(g_ffn), w_up.astype(_BF16), conv_w.astype(_F32), row(conv_b),
                    w_down.astype(_BF16), row(g_final), tm=tm, tiles_per_seq=s // tm)
    return out.reshape(b, s, d)
```

```python
import functools
import math

import jax
import jax.numpy as jnp
import numpy as np
from jax import lax
from jax.experimental import pallas as pl
from jax.experimental.pallas import tpu as pltpu

SB_HEADS = 8
SW_HEADS = 8
SW_KV_HEADS = 2
HEAD_DIM = 64
BLOCK = 128
NUM_BUCKETS = 32
MAX_DISTANCE = 128
CONV_WIDTH = 3
EPS = 1e-6
MASKED = float("-inf")
LANES = 128
V7X_MXU_DIM = 256
V7X_VMEM_BYTES = 64 * 1024 * 1024
VMEM_LIMIT = V7X_VMEM_BYTES * 7 // 8
ROW_TILE = 512
IN_CHUNK = 2 * V7X_MXU_DIM
FF_CHUNK = V7X_MXU_DIM

_BF16 = jnp.bfloat16
_F32 = jnp.float32


def _rms(x, g):
    return x * lax.rsqrt(jnp.mean(x * x, axis=-1, keepdims=True) + EPS) * g


def _dot(a, b):
    return jnp.dot(a, b, preferred_element_type=_F32)


def _dot_nt(a, b):
    return lax.dot_general(a, b, (((1,), (1,)), ((), ())), preferred_element_type=_F32)


ROW_SUB = 2


def _in_proj_kernel(x_ref, g_ref, w_ref, o_ref, h_scr, *, n_scaled, chunk):
    tm = x_ref.shape[0] // ROW_SUB
    rows = lambda s: slice(s * tm, (s + 1) * tm)
    scale = 1.0 / math.sqrt(HEAD_DIM)

    def prologue(s):
        h_scr[rows(s), :] = _rms(x_ref[rows(s), :], g_ref[...]).astype(_BF16)

    prologue(0)
    width = w_ref.shape[1]
    for s in range(ROW_SUB):
        for c, start in enumerate(range(0, width, chunk)):
            cols = slice(start, min(start + chunk, width))
            p = _dot(h_scr[rows(s), :], w_ref[:, cols])
            if c in n_scaled:
                p = p * scale
            o_ref[rows(s), cols] = p.astype(_BF16)
            if c == 0 and s + 1 < ROW_SUB:
                prologue(s + 1)


def _in_proj(x2d, g, w, *, tm, n_scaled, chunk):
    n, d = x2d.shape
    width = w.shape[1]
    step_rows = ROW_SUB * tm
    return pl.pallas_call(
        functools.partial(_in_proj_kernel, n_scaled=n_scaled, chunk=chunk),
        grid=(n // step_rows,),
        in_specs=[
            pl.BlockSpec((step_rows, d), lambda i: (i, 0)),
            pl.BlockSpec((1, d), lambda i: (0, 0), pipeline_mode=pl.Buffered(1)),
            pl.BlockSpec((d, width), lambda i: (0, 0), pipeline_mode=pl.Buffered(1)),
        ],
        out_specs=pl.BlockSpec((step_rows, width), lambda i: (i, 0)),
        out_shape=jax.ShapeDtypeStruct((n, width), _BF16),
        scratch_shapes=[pltpu.VMEM((step_rows, d), _BF16)],
        compiler_params=pltpu.CompilerParams(
            dimension_semantics=("arbitrary",), vmem_limit_bytes=VMEM_LIMIT),
        name="in_proj",
    )(x2d, g, w)


SB_SUB = 32
SB_TOP = 32
SB_MASKED = -1e30
SB_SKEW = 3
SB_OFF = 1e30
LOG2E = 1.4426950408889634
SB_DEAD_LOG2 = 151.0


def _sb_kernel(q_ref, k_ref, v_ref, t_ref, o_ref, rem_scr, acc_scr):
    base = pl.program_id(2) * SB_SUB
    tri = t_ref[...]
    rows2, top2 = 2 * BLOCK, 2 * SB_TOP
    rest = BLOCK - SB_TOP
    lane = lax.broadcasted_iota(jnp.int32, (rows2, LANES), 1)
    row = lax.broadcasted_iota(jnp.int32, (rows2, LANES), 0)
    head1 = ((row >= SB_TOP) & (row < top2)) | (row >= top2 + rest)
    own_lanes = (lane >= HEAD_DIM) == head1
    query = jnp.where(row < SB_TOP, row, jnp.where(row < top2 + rest, row - SB_TOP, row - BLOCK))
    strictly_earlier = lane < query

    def stacked_q(a):
        q = q_ref[0, a * BLOCK:(a + 1) * BLOCK, :]
        q2 = jnp.concatenate([q[:SB_TOP], q[:SB_TOP], q[SB_TOP:], q[SB_TOP:]], axis=0)
        return jnp.where(own_lanes, q2, jnp.zeros_like(q2))

    def key_rows(kb):
        return pl.ds(pl.multiple_of(kb * BLOCK, BLOCK), BLOCK)

    def scores(q2, kb, diagonal):
        z = _dot_nt(q2, k_ref[0, key_rows(kb), :]) * LOG2E
        return jnp.where(strictly_earlier, z, SB_MASKED) if diagonal else z

    def suffix_mass(z):
        sign = jnp.uint32(0x80000000)
        neg_abs = lax.bitcast_convert_type(lax.bitcast_convert_type(z, jnp.uint32) | sign, _F32)
        sp = jnp.maximum(z, 0.0) + jnp.log2(1.0 + jnp.exp2(neg_abs))
        hi = sp.astype(_BF16)
        lo = (sp - hi.astype(_F32)).astype(_BF16)
        return _dot(jnp.concatenate([hi, lo], axis=1), tri)

    def weighted_values(z, cum, rem, kb):
        arg = z - cum[:, :BLOCK]
        w = jnp.exp2(arg if rem is None else arg - rem)
        return _dot(w.astype(_BF16), v_ref[0, key_rows(kb), :])

    def earlier_block(a, t, rem):
        kb = base + a - t
        return jnp.maximum(kb, 0), rem + jnp.where(kb < 0, SB_OFF, 0.0)

    items = [(a, t, slice(0, top2) if t == 2 else slice(0, rows2))
             for a in range(SB_SUB) for t in range(3)]
    q2s, zs, cums = {}, {}, {}
    rems, accs = [None] * SB_SUB, [None] * SB_SUB
    for step in range(len(items) + 2 * SB_SKEW):
        if step < len(items):
            a, t, part = items[step]
            if t == 0:
                q2s[a] = stacked_q(a)
            zs[step] = scores(q2s[a][part], jnp.maximum(base + a - t, 0), t == 0)
        i = step - SB_SKEW
        if 0 <= i < len(items):
            cums[i] = suffix_mass(zs[i])
        i = step - 2 * SB_SKEW
        if 0 <= i < len(items):
            a, t, part = items[i]
            if t == 0:
                kb, rem = base + a, None
            elif a >= t:
                kb, rem = base + a - t, rems[a][part]
            else:
                kb, rem = earlier_block(a, t, rems[a][part])
            acc = weighted_values(zs.pop(i), cums[i], rem, kb)
            total = cums.pop(i)[:, BLOCK:]
            if t > 0:
                rem, acc = rem + total, accs[a][part] + acc
            else:
                rem = total
            if part.stop < rows2:
                rem = jnp.concatenate([rem, rems[a][part.stop:]], axis=0)
                acc = jnp.concatenate([acc, accs[a][part.stop:]], axis=0)
            rems[a], accs[a] = rem, acc
    floor = None
    for a in range(SB_SUB):
        rem_scr[a] = rems[a]
        acc_scr[a] = accs[a]
        floor = rems[a] if floor is None else jnp.minimum(floor, rems[a])

    def cond(carry):
        t, least = carry
        return (base + SB_SUB - 1 - t >= 0) & (least < SB_DEAD_LOG2)

    def body(carry):
        t, _ = carry
        floor = None
        for a in range(SB_SUB):
            q2 = stacked_q(a)
            for part, back in ((slice(top2, rows2), t), (slice(0, top2), t + 1)):
                kb, rem = earlier_block(a, back, rem_scr[a, part, :])
                z = scores(q2[part], kb, False)
                cum = suffix_mass(z)
                acc_scr[a, part, :] += weighted_values(z, cum, rem, kb)
                rem = rem + cum[:, BLOCK:]
                rem_scr[a, part, :] = rem
                least = jnp.min(rem, axis=0, keepdims=True)
                floor = least if floor is None else jnp.minimum(floor, least)
        return t + 1, jnp.min(floor)

    lax.while_loop(cond, body, (jnp.int32(2), jnp.min(floor)))
    head0_lanes = lax.broadcasted_iota(jnp.int32, (BLOCK, LANES), 1) < HEAD_DIM
    for a in range(SB_SUB):
        head0 = jnp.concatenate([acc_scr[a, :SB_TOP, :], acc_scr[a, top2:top2 + rest, :]], axis=0)
        head1 = jnp.concatenate([acc_scr[a, SB_TOP:top2, :], acc_scr[a, top2 + rest:, :]], axis=0)
        o_ref[0, a * BLOCK:(a + 1) * BLOCK, :] = jnp.where(
            head0_lanes, head0, head1).astype(o_ref.dtype)


def _suffix_sum_matrix():
    j = np.arange(2 * BLOCK)[:, None] % BLOCK
    s = np.arange(2 * BLOCK)[None, :]
    return jnp.asarray(((s >= BLOCK) | (j >= s)).astype(np.float32), dtype=_BF16)


def _sb_attn(qkv, *, q_col, k_col, v_col):
    b, s, _ = qkv.shape
    pairs = SB_HEADS * HEAD_DIM // LANES
    tq = SB_SUB * BLOCK
    assert s % tq == 0
    return pl.pallas_call(
        _sb_kernel,
        grid=(b, pairs, s // tq),
        in_specs=[
            pl.BlockSpec((1, tq, LANES), lambda bi, p, qi: (bi, qi, q_col + p)),
            pl.BlockSpec((1, s, LANES), lambda bi, p, qi: (bi, 0, k_col + p)),
            pl.BlockSpec((1, s, LANES), lambda bi, p, qi: (bi, 0, v_col + p)),
            pl.BlockSpec((2 * BLOCK, 2 * BLOCK), lambda bi, p, qi: (0, 0)),
        ],
        out_specs=pl.BlockSpec((1, tq, LANES), lambda bi, p, qi: (bi, qi, p)),
        out_shape=jax.ShapeDtypeStruct((b, s, SB_HEADS * HEAD_DIM), _BF16),
        scratch_shapes=[
            pltpu.VMEM((SB_SUB, 2 * BLOCK, BLOCK), _F32),
            pltpu.VMEM((SB_SUB, 2 * BLOCK, LANES), _F32),
        ],
        compiler_params=pltpu.CompilerParams(
            dimension_semantics=("arbitrary", "arbitrary", "arbitrary"),
            vmem_limit_bytes=VMEM_LIMIT),
        name="sb_attn",
    )(qkv, qkv, qkv, _suffix_sum_matrix())


def _sw_bias_kernel(bucket_ref, rel_ref, sink_ref, o_ref):
    bucket = bucket_ref[...]
    col = lax.broadcasted_iota(jnp.int32, bucket.shape, 1)
    dist = lax.broadcasted_iota(jnp.int32, bucket.shape, 0) + BLOCK - col
    in_window = (dist >= 0) & (dist < BLOCK)
    for h in range(SW_HEADS):
        acc = jnp.zeros(bucket.shape, _F32)
        for b in range(NUM_BUCKETS):
            acc = jnp.where(bucket == b, rel_ref[b, h], acc)
        o_ref[h // SW_GROUP, (h % SW_GROUP) * BLOCK:(h % SW_GROUP + 1) * BLOCK, :] = jnp.where(
            col == 0, sink_ref[h], jnp.where(in_window, acc, MASKED))


def _t5_bucket(dist):
    max_exact = NUM_BUCKETS // 2
    d = np.maximum(dist, 1).astype(np.float32)
    large = max_exact + (np.log(d / max_exact) / math.log(MAX_DISTANCE / max_exact)
                         * (NUM_BUCKETS - max_exact)).astype(np.int32)
    large = np.minimum(large, NUM_BUCKETS - 1)
    return np.where(dist < max_exact, dist, large).astype(np.int32)


def _sw_bias(rel_bias, sinks):
    dist = (np.arange(BLOCK)[:, None] + BLOCK) - np.arange(2 * BLOCK)[None, :]
    bucket = jnp.asarray(_t5_bucket(np.maximum(dist, 0)))
    return pl.pallas_call(
        _sw_bias_kernel,
        in_specs=[
            pl.BlockSpec(memory_space=pltpu.VMEM),
            pl.BlockSpec(memory_space=pltpu.SMEM),
            pl.BlockSpec(memory_space=pltpu.SMEM),
        ],
        out_specs=pl.BlockSpec(memory_space=pltpu.VMEM),
        out_shape=jax.ShapeDtypeStruct((SW_KV_HEADS, SW_GROUP * BLOCK, 2 * BLOCK), _F32),
        name="sw_bias",
    )(bucket, rel_bias.astype(_F32), sinks.astype(_F32))


SW_SUB = 32
SW_GROUP = SW_HEADS // SW_KV_HEADS


def _sw_kernel(q_ref, kp_ref, kc_ref, vp_ref, vc_ref, bias_ref, o_ref):
    rows = SW_GROUP * BLOCK
    upper_lanes = lax.broadcasted_iota(jnp.int32, (rows, LANES), 1) >= HEAD_DIM
    kv_upper = lax.broadcasted_iota(jnp.int32, (2 * BLOCK, LANES), 1) >= HEAD_DIM
    not_sink = lax.broadcasted_iota(jnp.int32, (2 * BLOCK, LANES), 0) > 0
    kpos = lax.broadcasted_iota(jnp.int32, (rows, 2 * BLOCK), 1)
    first_ok = (kpos >= jnp.where(pl.program_id(1) > 0, 0, BLOCK)) | (kpos == 0)

    def kv_rows(prev_ref, cur_ref, blk, keep):
        cur = cur_ref[0, blk * BLOCK:(blk + 1) * BLOCK, :]
        prev = prev_ref[0] if blk == 0 else cur_ref[0, (blk - 1) * BLOCK:blk * BLOCK, :]
        kv = jnp.concatenate([prev, cur], axis=0)
        return jnp.where(keep, kv, jnp.zeros_like(kv))

    def logits(blk, g):
        q4 = jnp.concatenate(
            [q_ref[0, blk * BLOCK:(blk + 1) * BLOCK, t * LANES:(t + 1) * LANES]
             for t in range(SW_GROUP)], axis=0)
        keys = kv_rows(kp_ref, kc_ref, blk, not_sink & (kv_upper if g == 1 else ~kv_upper))
        return _dot_nt(q4, keys)

    def exponentials(z, blk, g):
        z = z + bias_ref[g]
        if blk == 0:
            z = jnp.where(first_ok, z, MASKED)
        return jnp.exp(z - jnp.max(z, axis=-1, keepdims=True)).astype(_BF16)

    def values(p, blk):
        vv = kv_rows(vp_ref, vc_ref, blk, not_sink)
        pv = _dot(p, jnp.concatenate([vv, jnp.ones_like(vv)], axis=1))
        return pv[:, :LANES] * (1.0 / pv[:, LANES:])

    items = [(blk, g) for blk in range(SW_SUB) for g in range(SW_KV_HEADS)]
    zs, ps, outs = {}, {}, {}
    for step in range(len(items) + 2):
        if step < len(items):
            zs[step] = logits(*items[step])
        if 0 <= step - 1 < len(items):
            ps[step - 1] = exponentials(zs.pop(step - 1), *items[step - 1])
        if 0 <= step - 2 < len(items):
            blk, g = items[step - 2]
            outs[g] = values(ps.pop(step - 2), blk)
            if g == SW_KV_HEADS - 1:
                y = jnp.where(upper_lanes, outs.pop(1), outs.pop(0)).astype(o_ref.dtype)
                for t in range(SW_GROUP):
                    o_ref[0, blk * BLOCK:(blk + 1) * BLOCK, t * LANES:(t + 1) * LANES] = (
                        y[t * BLOCK:(t + 1) * BLOCK])


def _sw_attn(qkv, table, *, q_col, k_col, v_col):
    b, s, _ = qkv.shape
    qw = SW_HEADS * HEAD_DIM
    tq = SW_SUB * BLOCK
    assert s % tq == 0 and SW_KV_HEADS * HEAD_DIM == LANES
    prev = lambda bi, n: (bi, jnp.maximum(n * SW_SUB - 1, 0))
    return pl.pallas_call(
        _sw_kernel,
        grid=(b, s // tq),
        in_specs=[
            pl.BlockSpec((1, tq, qw), lambda bi, n: (bi, n, q_col)),
            pl.BlockSpec((1, BLOCK, LANES), lambda bi, n: prev(bi, n) + (k_col,)),
            pl.BlockSpec((1, tq, LANES), lambda bi, n: (bi, n, k_col)),
            pl.BlockSpec((1, BLOCK, LANES), lambda bi, n: prev(bi, n) + (v_col,)),
            pl.BlockSpec((1, tq, LANES), lambda bi, n: (bi, n, v_col)),
            pl.BlockSpec((SW_KV_HEADS, SW_GROUP * BLOCK, 2 * BLOCK), lambda bi, n: (0, 0, 0)),
        ],
        out_specs=pl.BlockSpec((1, tq, qw), lambda bi, n: (bi, n, 0)),
        out_shape=jax.ShapeDtypeStruct((b, s, qw), _BF16),
        compiler_params=pltpu.CompilerParams(
            dimension_semantics=("arbitrary", "arbitrary"), vmem_limit_bytes=VMEM_LIMIT),
        name="sw_attn",
    )(qkv, qkv, qkv, qkv, qkv, table)


MIX_CHUNK = V7X_MXU_DIM


def _mix_kernel(x_ref, g_ref, wg_ref, ysb_ref, ysw_ref, wsb_ref, wsw_ref, wo_ref, o_ref,
                h_scr, merged_scr):
    d = x_ref.shape[1]
    tm = x_ref.shape[0] // ROW_SUB
    n_chunks = d // MIX_CHUNK
    rows = lambda s: slice(s * tm, (s + 1) * tm)

    def prologue(s):
        h_scr[rows(s), :] = _rms(x_ref[rows(s), :], g_ref[...]).astype(_BF16)

    def projections(s, c):
        cols = slice(c * MIX_CHUNK, (c + 1) * MIX_CHUNK)
        gate_cols = slice(d + c * MIX_CHUNK, d + (c + 1) * MIX_CHUNK)
        h = h_scr[rows(s), :]
        return (_dot(h, wg_ref[:, cols]), _dot(ysb_ref[rows(s), :], wsb_ref[:, cols]),
                _dot(h, wg_ref[:, gate_cols]), _dot(ysw_ref[rows(s), :], wsw_ref[:, cols]))

    def epilogue(s, y):
        o_ref[rows(s), :] = x_ref[rows(s), :] + y

    prologue(0)
    ahead = projections(0, 0)
    pending = None
    for s in range(ROW_SUB):
        for c in range(n_chunks):
            logit_sb, y_sb, logit_sw, y_sw = ahead
            if c + 1 < n_chunks:
                ahead = projections(s, c + 1)
            elif s + 1 < ROW_SUB:
                ahead = projections(s + 1, 0)
            merged = jax.nn.sigmoid(logit_sb) * y_sb + jax.nn.sigmoid(logit_sw) * y_sw
            merged_scr[rows(s), c * MIX_CHUNK:(c + 1) * MIX_CHUNK] = merged.astype(_BF16)
            if c == 0 and s + 1 < ROW_SUB:
                prologue(s + 1)
            if c == 1 and pending is not None:
                epilogue(*pending)
                pending = None
        pending = (s, _dot(merged_scr[rows(s), :], wo_ref[...]))
    epilogue(*pending)


def _mix_out(x2d, g, w_gate, y_sb, y_sw, w_sb, w_sw, w_out, *, tm):
    n, d = x2d.shape
    assert d % MIX_CHUNK == 0
    step_rows = ROW_SUB * tm
    full = lambda a: pl.BlockSpec(a.shape, lambda i: (0, 0), pipeline_mode=pl.Buffered(1))
    rows = lambda a: pl.BlockSpec((step_rows, a.shape[1]), lambda i: (i, 0))
    return pl.pallas_call(
        _mix_kernel,
        grid=(n // step_rows,),
        in_specs=[rows(x2d), full(g), full(w_gate), rows(y_sb), rows(y_sw),
                  full(w_sb), full(w_sw), full(w_out)],
        out_specs=pl.BlockSpec((step_rows, d), lambda i: (i, 0)),
        out_shape=jax.ShapeDtypeStruct((n, d), _F32),
        scratch_shapes=[pltpu.VMEM((step_rows, d), _BF16), pltpu.VMEM((step_rows, d), _BF16)],
        compiler_params=pltpu.CompilerParams(
            dimension_semantics=("arbitrary",), vmem_limit_bytes=VMEM_LIMIT),
        name="mix_out",
    )(x2d, g, w_gate, y_sb, y_sw, w_sb, w_sw, w_out)


FFN_SUB = 2
HALO = 8


def _ffn_kernel(x_ref, g_ref, wu_ref, cw_ref, cb_ref, wd_ref, gf_ref, o_ref,
                h_scr, act_scr, u_scr, carry_scr, *, steps_per_seq):
    tm = x_ref.shape[0] // FFN_SUB
    d_ff = wd_ref.shape[0]
    nj = d_ff // FF_CHUNK
    rows = lambda s: slice(s * tm, (s + 1) * tm)

    @pl.when(pl.program_id(0) % steps_per_seq == 0)
    def _():
        carry_scr[...] = jnp.zeros_like(carry_scr)

    def prologue(s):
        h_scr[rows(s), :] = _rms(x_ref[rows(s), :], g_ref[...]).astype(_BF16)

    def up_proj(s, j):
        h = h_scr[rows(s), :]
        gate = _dot(h, wu_ref[:, j * FF_CHUNK:(j + 1) * FF_CHUNK])
        value = _dot(h, wu_ref[:, d_ff + j * FF_CHUNK:d_ff + (j + 1) * FF_CHUNK])
        return gate, value

    def conv(slot, j, part, u):
        slabs = []
        for k in range(FF_CHUNK // LANES):
            slab = (slot * 2 + part) * (FF_CHUNK // LANES) + k
            cols = slice(part * d_ff + j * FF_CHUNK + k * LANES,
                         part * d_ff + j * FF_CHUNK + (k + 1) * LANES)
            lanes = slice((part * FF_CHUNK // LANES + k) * LANES,
                          (part * FF_CHUNK // LANES + k + 1) * LANES)
            uk = u[:, k * LANES:(k + 1) * LANES]
            u_scr[slab, HALO:HALO + tm, :] = uk
            u_scr[slab, 0:HALO, :] = carry_scr[j, :, lanes]
            carry_scr[j, :, lanes] = uk[tm - HALO:tm, :]
            y = cb_ref[:, cols] + cw_ref[CONV_WIDTH - 1:CONV_WIDTH, cols] * uk
            for tap in range(CONV_WIDTH - 1):
                back = CONV_WIDTH - 1 - tap
                y = y + cw_ref[tap:tap + 1, cols] * u_scr[slab, pl.ds(HALO - back, tm, stride=1), :]
            slabs.append(y)
        return jnp.concatenate(slabs, axis=1)

    def gated_conv(s, j, gate, value):
        slot = (s * nj + j) % 2
        act = jax.nn.silu(conv(slot, j, 0, gate)) * conv(slot, j, 1, value)
        act_scr[rows(s), j * FF_CHUNK:(j + 1) * FF_CHUNK] = act.astype(_BF16)

    def epilogue(s, y):
        o_ref[rows(s), :] = _rms(x_ref[rows(s), :] + y, gf_ref[...])

    prologue(0)
    ahead = up_proj(0, 0)
    pending = None
    for s in range(FFN_SUB):
        for j in range(nj):
            gate, value = ahead
            if j + 1 < nj:
                ahead = up_proj(s, j + 1)
            elif s + 1 < FFN_SUB:
                ahead = up_proj(s + 1, 0)
            gated_conv(s, j, gate, value)
            if j == 0 and s + 1 < FFN_SUB:
                prologue(s + 1)
            if j == 1 and pending is not None:
                epilogue(*pending)
                pending = None
        pending = (s, _dot(act_scr[rows(s), :], wd_ref[...]))
    epilogue(*pending)


def _conv_ffn(x1, g_ffn, w_up, conv_w, conv_b, w_down, g_final, *, tm, tiles_per_seq):
    n, d = x1.shape
    d_ff = w_down.shape[0]
    nj = d_ff // FF_CHUNK
    assert tiles_per_seq % FFN_SUB == 0
    step_rows = FFN_SUB * tm
    resident = lambda a: pl.BlockSpec(a.shape, lambda i: (0, 0), pipeline_mode=pl.Buffered(1))
    return pl.pallas_call(
        functools.partial(_ffn_kernel, steps_per_seq=tiles_per_seq // FFN_SUB),
        grid=(n // step_rows,),
        in_specs=[
            pl.BlockSpec((step_rows, d), lambda i: (i, 0)),
            resident(g_ffn), resident(w_up), resident(conv_w), resident(conv_b),
            resident(w_down), resident(g_final),
        ],
        out_specs=pl.BlockSpec((step_rows, d), lambda i: (i, 0)),
        out_shape=jax.ShapeDtypeStruct((n, d), _F32),
        scratch_shapes=[
            pltpu.VMEM((step_rows, d), _BF16),
            pltpu.VMEM((step_rows, d_ff), _BF16),
            pltpu.VMEM((2 * 2 * FF_CHUNK // LANES, HALO + tm, LANES), _F32),
            pltpu.VMEM((nj, HALO, 2 * FF_CHUNK), _F32),
        ],
        compiler_params=pltpu.CompilerParams(
            dimension_semantics=("arbitrary",), vmem_limit_bytes=VMEM_LIMIT),
        name="conv_ffn",
    )(x1, g_ffn, w_up, conv_w, conv_b, w_down, g_final)


def kernel(x, g_mix, w_in, w_sb_proj, w_sw_proj, w_out, rel_bias, sinks, g_ffn, w_up,
           conv_w, conv_b, w_down, g_final):
    b, s, d = x.shape
    n = b * s
    d_ff = w_down.shape[0]
    sb_w = SB_HEADS * HEAD_DIM
    sw_qw = SW_HEADS * HEAD_DIM
    sw_kw = SW_KV_HEADS * HEAD_DIM
    assert s % BLOCK == 0 and d_ff % FF_CHUNK == 0
    tm = min(ROW_TILE, s)
    assert s % (max(ROW_SUB, FFN_SUB) * tm) == 0

    qkv_end = 3 * sb_w + sw_qw
    order = np.arange(SW_HEADS).reshape(SW_KV_HEADS, SW_GROUP).T.reshape(-1)
    by_head = lambda a, axis: jnp.take(
        a.reshape(a.shape[:axis] + (SW_HEADS, HEAD_DIM) + a.shape[axis + 1:]), order,
        axis=axis).reshape(a.shape)
    w_in = w_in.astype(_BF16)
    w_qkv = jnp.concatenate(
        [w_in[:, :3 * sb_w], by_head(w_in[:, 3 * sb_w:qkv_end], 1),
         w_in[:, qkv_end:qkv_end + 2 * sw_kw]], axis=1)
    w_gate = w_in[:, qkv_end + 2 * sw_kw:]
    row = lambda a: a.reshape(1, -1).astype(_F32)

    x2d = x.reshape(n, d)
    assert sb_w == IN_CHUNK and sw_qw == IN_CHUNK
    qkv = _in_proj(x2d, row(g_mix), w_qkv, tm=tm, chunk=IN_CHUNK,
                   n_scaled=(0, 3 * sb_w // IN_CHUNK))
    qkv = qkv.reshape(b, s, -1)
    y_sb = _sb_attn(qkv, q_col=0, k_col=sb_w // LANES, v_col=2 * sb_w // LANES)
    y_sw = _sw_attn(qkv, _sw_bias(rel_bias, sinks), q_col=3 * sb_w // sw_qw,
                    k_col=qkv_end // LANES, v_col=qkv_end // LANES + 1)
    x1 = _mix_out(x2d, row(g_mix), w_gate, y_sb.reshape(n, sb_w), y_sw.reshape(n, sw_qw),
                  w_sb_proj.astype(_BF16), by_head(w_sw_proj, 0).astype(_BF16),
                  w_out.astype(_BF16), tm=tm)
    out = _conv_ffn(x1, row(g_ffn), w_up.astype(_BF16), conv_w.astype(_F32), row(conv_b),
                    w_down.astype(_BF16), row(g_final), tm=tm, tiles_per_seq=s // tm)
    return out.reshape(b, s, d)
```

```python
import functools
import math

import jax
import jax.numpy as jnp
import numpy as np
from jax import lax
from jax.experimental import pallas as pl
from jax.experimental.pallas import tpu as pltpu

SB_HEADS = 8
SW_HEADS = 8
SW_KV_HEADS = 2
HEAD_DIM = 64
BLOCK = 128
NUM_BUCKETS = 32
MAX_DISTANCE = 128
CONV_WIDTH = 3
EPS = 1e-6
MASKED = float("-inf")
LANES = 128
V7X_MXU_DIM = 256
V7X_VMEM_BYTES = 64 * 1024 * 1024
VMEM_LIMIT = V7X_VMEM_BYTES * 7 // 8
ROW_TILE = 512
IN_CHUNK = 2 * V7X_MXU_DIM
FF_CHUNK = V7X_MXU_DIM

_BF16 = jnp.bfloat16
_F32 = jnp.float32


def _rms(x, g):
    return x * lax.rsqrt(jnp.mean(x * x, axis=-1, keepdims=True) + EPS) * g


def _dot(a, b):
    return jnp.dot(a, b, preferred_element_type=_F32)


def _dot_nt(a, b):
    return lax.dot_general(a, b, (((1,), (1,)), ((), ())), preferred_element_type=_F32)


ROW_SUB = 2


def _in_proj_kernel(x_ref, g_ref, w_ref, o_ref, h_scr, *, n_scaled, chunk):
    tm = x_ref.shape[0] // ROW_SUB
    rows = lambda s: slice(s * tm, (s + 1) * tm)
    scale = 1.0 / math.sqrt(HEAD_DIM)

    def prologue(s):
        h_scr[rows(s), :] = _rms(x_ref[rows(s), :], g_ref[...]).astype(_BF16)

    prologue(0)
    width = w_ref.shape[1]
    for s in range(ROW_SUB):
        for c, start in enumerate(range(0, width, chunk)):
            cols = slice(start, min(start + chunk, width))
            p = _dot(h_scr[rows(s), :], w_ref[:, cols])
            if c in n_scaled:
                p = p * scale
            o_ref[rows(s), cols] = p.astype(_BF16)
            if c == 0 and s + 1 < ROW_SUB:
                prologue(s + 1)


def _in_proj(x2d, g, w, *, tm, n_scaled, chunk):
    n, d = x2d.shape
    width = w.shape[1]
    step_rows = ROW_SUB * tm
    return pl.pallas_call(
        functools.partial(_in_proj_kernel, n_scaled=n_scaled, chunk=chunk),
        grid=(n // step_rows,),
        in_specs=[
            pl.BlockSpec((step_rows, d), lambda i: (i, 0)),
            pl.BlockSpec((1, d), lambda i: (0, 0), pipeline_mode=pl.Buffered(1)),
            pl.BlockSpec((d, width), lambda i: (0, 0), pipeline_mode=pl.Buffered(1)),
        ],
        out_specs=pl.BlockSpec((step_rows, width), lambda i: (i, 0)),
        out_shape=jax.ShapeDtypeStruct((n, width), _BF16),
        scratch_shapes=[pltpu.VMEM((step_rows, d), _BF16)],
        compiler_params=pltpu.CompilerParams(
            dimension_semantics=("arbitrary",), vmem_limit_bytes=VMEM_LIMIT),
        name="in_proj",
    )(x2d, g, w)


SB_SUB = 32
SB_TOP = 32
SB_MASKED = -1e30
SB_SKEW = 3
SB_OFF = 1e30
LOG2E = 1.4426950408889634
SB_DEAD_LOG2 = 151.0


def _sb_kernel(q_ref, k_ref, v_ref, t_ref, o_ref, rem_scr, acc_scr):
    base = pl.program_id(2) * SB_SUB
    tri = t_ref[...]
    rows2, top2 = 2 * BLOCK, 2 * SB_TOP
    rest = BLOCK - SB_TOP
    lane = lax.broadcasted_iota(jnp.int32, (rows2, LANES), 1)
    row = lax.broadcasted_iota(jnp.int32, (rows2, LANES), 0)
    head1 = ((row >= SB_TOP) & (row < top2)) | (row >= top2 + rest)
    own_lanes = (lane >= HEAD_DIM) == head1
    query = jnp.where(row < SB_TOP, row, jnp.where(row < top2 + rest, row - SB_TOP, row - BLOCK))
    strictly_earlier = lane < query

    def stacked_q(a):
        q = q_ref[0, a * BLOCK:(a + 1) * BLOCK, :]
        q2 = jnp.concatenate([q[:SB_TOP], q[:SB_TOP], q[SB_TOP:], q[SB_TOP:]], axis=0)
        return jnp.where(own_lanes, q2, jnp.zeros_like(q2))

    def key_rows(kb):
        return pl.ds(pl.multiple_of(kb * BLOCK, BLOCK), BLOCK)

    def scores(q2, kb, diagonal):
        z = _dot_nt(q2, k_ref[0, key_rows(kb), :]) * LOG2E
        return jnp.where(strictly_earlier, z, SB_MASKED) if diagonal else z

    def suffix_mass(z):
        sign = jnp.uint32(0x80000000)
        neg_abs = lax.bitcast_convert_type(lax.bitcast_convert_type(z, jnp.uint32) | sign, _F32)
        sp = jnp.maximum(z, 0.0) + jnp.log2(1.0 + jnp.exp2(neg_abs))
        hi = sp.astype(_BF16)
        lo = (sp - hi.astype(_F32)).astype(_BF16)
        cum = _dot(jnp.concatenate([hi, lo], axis=1), tri)
        return cum, jnp.broadcast_to(cum[:, 0:1], cum.shape)

    def weighted_values(z, cum, rem, kb):
        arg = z - cum
        w = jnp.exp2(arg if rem is None else arg - rem)
        return _dot(w.astype(_BF16), v_ref[0, key_rows(kb), :])

    def earlier_block(a, t, rem):
        kb = base + a - t
        return jnp.maximum(kb, 0), rem + jnp.where(kb < 0, SB_OFF, 0.0)

    items = [(a, t, slice(0, top2) if t == 2 else slice(0, rows2))
             for a in range(SB_SUB) for t in range(3)]
    q2s, zs, cums = {}, {}, {}
    rems, accs = [None] * SB_SUB, [None] * SB_SUB
    for step in range(len(items) + 2 * SB_SKEW):
        if step < len(items):
            a, t, part = items[step]
            if t == 0:
                q2s[a] = stacked_q(a)
            zs[step] = scores(q2s[a][part], jnp.maximum(base + a - t, 0), t == 0)
        i = step - SB_SKEW
        if 0 <= i < len(items):
            cums[i] = suffix_mass(zs[i])
        i = step - 2 * SB_SKEW
        if 0 <= i < len(items):
            a, t, part = items[i]
            if t == 0:
                kb, rem = base + a, None
            elif a >= t:
                kb, rem = base + a - t, rems[a][part]
            else:
                kb, rem = earlier_block(a, t, rems[a][part])
            cum, total = cums.pop(i)
            acc = weighted_values(zs.pop(i), cum, rem, kb)
            if t > 0:
                rem, acc = rem + total, accs[a][part] + acc
            else:
                rem = total
            if part.stop < rows2:
                rem = jnp.concatenate([rem, rems[a][part.stop:]], axis=0)
                acc = jnp.concatenate([acc, accs[a][part.stop:]], axis=0)
            rems[a], accs[a] = rem, acc
    floor = None
    for a in range(SB_SUB):
        rem_scr[a] = rems[a]
        acc_scr[a] = accs[a]
        floor = rems[a] if floor is None else jnp.minimum(floor, rems[a])

    def cond(carry):
        t, least = carry
        return (base + SB_SUB - 1 - t >= 0) & (least < SB_DEAD_LOG2)

    def body(carry):
        t, _ = carry
        floor = None
        for a in range(SB_SUB):
            q2 = stacked_q(a)
            for part, back in ((slice(top2, rows2), t), (slice(0, top2), t + 1)):
                kb, rem = earlier_block(a, back, rem_scr[a, part, :])
                z = scores(q2[part], kb, False)
                cum, total = suffix_mass(z)
                acc_scr[a, part, :] += weighted_values(z, cum, rem, kb)
                rem = rem + total
                rem_scr[a, part, :] = rem
                least = jnp.min(rem, axis=0, keepdims=True)
                floor = least if floor is None else jnp.minimum(floor, least)
        return t + 1, jnp.min(floor)

    lax.while_loop(cond, body, (jnp.int32(2), jnp.min(floor)))
    head0_lanes = lax.broadcasted_iota(jnp.int32, (BLOCK, LANES), 1) < HEAD_DIM
    for a in range(SB_SUB):
        head0 = jnp.concatenate([acc_scr[a, :SB_TOP, :], acc_scr[a, top2:top2 + rest, :]], axis=0)
        head1 = jnp.concatenate([acc_scr[a, SB_TOP:top2, :], acc_scr[a, top2 + rest:, :]], axis=0)
        o_ref[0, a * BLOCK:(a + 1) * BLOCK, :] = jnp.where(
            head0_lanes, head0, head1).astype(o_ref.dtype)


def _suffix_sum_matrix():
    j = np.arange(2 * BLOCK)[:, None] % BLOCK
    s = np.arange(BLOCK)[None, :]
    return jnp.asarray((j >= s).astype(np.float32), dtype=_BF16)


def _sb_attn(qkv, *, q_col, k_col, v_col):
    b, s, _ = qkv.shape
    pairs = SB_HEADS * HEAD_DIM // LANES
    tq = SB_SUB * BLOCK
    assert s % tq == 0
    return pl.pallas_call(
        _sb_kernel,
        grid=(b, pairs, s // tq),
        in_specs=[
            pl.BlockSpec((1, tq, LANES), lambda bi, p, qi: (bi, qi, q_col + p)),
            pl.BlockSpec((1, s, LANES), lambda bi, p, qi: (bi, 0, k_col + p)),
            pl.BlockSpec((1, s, LANES), lambda bi, p, qi: (bi, 0, v_col + p)),
            pl.BlockSpec((2 * BLOCK, BLOCK), lambda bi, p, qi: (0, 0)),
        ],
        out_specs=pl.BlockSpec((1, tq, LANES), lambda bi, p, qi: (bi, qi, p)),
        out_shape=jax.ShapeDtypeStruct((b, s, SB_HEADS * HEAD_DIM), _BF16),
        scratch_shapes=[
            pltpu.VMEM((SB_SUB, 2 * BLOCK, BLOCK), _F32),
            pltpu.VMEM((SB_SUB, 2 * BLOCK, LANES), _F32),
        ],
        compiler_params=pltpu.CompilerParams(
            dimension_semantics=("arbitrary", "arbitrary", "arbitrary"),
            vmem_limit_bytes=VMEM_LIMIT),
        name="sb_attn",
    )(qkv, qkv, qkv, _suffix_sum_matrix())


def _sw_bias_kernel(bucket_ref, rel_ref, sink_ref, o_ref):
    bucket = bucket_ref[...]
    col = lax.broadcasted_iota(jnp.int32, bucket.shape, 1)
    dist = lax.broadcasted_iota(jnp.int32, bucket.shape, 0) + BLOCK - col
    in_window = (dist >= 0) & (dist < BLOCK)
    for h in range(SW_HEADS):
        acc = jnp.zeros(bucket.shape, _F32)
        for b in range(NUM_BUCKETS):
            acc = jnp.where(bucket == b, rel_ref[b, h], acc)
        o_ref[h // SW_GROUP, (h % SW_GROUP) * BLOCK:(h % SW_GROUP + 1) * BLOCK, :] = jnp.where(
            col == 0, sink_ref[h], jnp.where(in_window, acc, MASKED))


def _t5_bucket(dist):
    max_exact = NUM_BUCKETS // 2
    d = np.maximum(dist, 1).astype(np.float32)
    large = max_exact + (np.log(d / max_exact) / math.log(MAX_DISTANCE / max_exact)
                         * (NUM_BUCKETS - max_exact)).astype(np.int32)
    large = np.minimum(large, NUM_BUCKETS - 1)
    return np.where(dist < max_exact, dist, large).astype(np.int32)


def _sw_bias(rel_bias, sinks):
    dist = (np.arange(BLOCK)[:, None] + BLOCK) - np.arange(2 * BLOCK)[None, :]
    bucket = jnp.asarray(_t5_bucket(np.maximum(dist, 0)))
    return pl.pallas_call(
        _sw_bias_kernel,
        in_specs=[
            pl.BlockSpec(memory_space=pltpu.VMEM),
            pl.BlockSpec(memory_space=pltpu.SMEM),
            pl.BlockSpec(memory_space=pltpu.SMEM),
        ],
        out_specs=pl.BlockSpec(memory_space=pltpu.VMEM),
        out_shape=jax.ShapeDtypeStruct((SW_KV_HEADS, SW_GROUP * BLOCK, 2 * BLOCK), _F32),
        name="sw_bias",
    )(bucket, rel_bias.astype(_F32), sinks.astype(_F32))


SW_SUB = 32
SW_GROUP = SW_HEADS // SW_KV_HEADS


def _sw_kernel(q_ref, kp_ref, kc_ref, vp_ref, vc_ref, bias_ref, o_ref):
    rows = SW_GROUP * BLOCK
    upper_lanes = lax.broadcasted_iota(jnp.int32, (rows, LANES), 1) >= HEAD_DIM
    kv_upper = lax.broadcasted_iota(jnp.int32, (2 * BLOCK, LANES), 1) >= HEAD_DIM
    not_sink = lax.broadcasted_iota(jnp.int32, (2 * BLOCK, LANES), 0) > 0
    kpos = lax.broadcasted_iota(jnp.int32, (rows, 2 * BLOCK), 1)
    first_ok = (kpos >= jnp.where(pl.program_id(1) > 0, 0, BLOCK)) | (kpos == 0)

    def kv_rows(prev_ref, cur_ref, blk, keep):
        cur = cur_ref[0, blk * BLOCK:(blk + 1) * BLOCK, :]
        prev = prev_ref[0] if blk == 0 else cur_ref[0, (blk - 1) * BLOCK:blk * BLOCK, :]
        kv = jnp.concatenate([prev, cur], axis=0)
        return jnp.where(keep, kv, jnp.zeros_like(kv))

    def logits(blk, g):
        q4 = jnp.concatenate(
            [q_ref[0, blk * BLOCK:(blk + 1) * BLOCK, t * LANES:(t + 1) * LANES]
             for t in range(SW_GROUP)], axis=0)
        keys = kv_rows(kp_ref, kc_ref, blk, not_sink & (kv_upper if g == 1 else ~kv_upper))
        return _dot_nt(q4, keys)

    def exponentials(z, blk, g):
        z = z + bias_ref[g]
        if blk == 0:
            z = jnp.where(first_ok, z, MASKED)
        return jnp.exp(z - jnp.max(z, axis=-1, keepdims=True)).astype(_BF16)

    def values(p, blk):
        vv = kv_rows(vp_ref, vc_ref, blk, not_sink)
        pv = _dot(p, jnp.concatenate([vv, jnp.ones_like(vv)], axis=1))
        return pv[:, :LANES] * (1.0 / pv[:, LANES:])

    items = [(blk, g) for blk in range(SW_SUB) for g in range(SW_KV_HEADS)]
    zs, ps, outs = {}, {}, {}
    for step in range(len(items) + 2):
        if step < len(items):
            zs[step] = logits(*items[step])
        if 0 <= step - 1 < len(items):
            ps[step - 1] = exponentials(zs.pop(step - 1), *items[step - 1])
        if 0 <= step - 2 < len(items):
            blk, g = items[step - 2]
            outs[g] = values(ps.pop(step - 2), blk)
            if g == SW_KV_HEADS - 1:
                y = jnp.where(upper_lanes, outs.pop(1), outs.pop(0)).astype(o_ref.dtype)
                for t in range(SW_GROUP):
                    o_ref[0, blk * BLOCK:(blk + 1) * BLOCK, t * LANES:(t + 1) * LANES] = (
                        y[t * BLOCK:(t + 1) * BLOCK])


def _sw_attn(qkv, table, *, q_col, k_col, v_col):
    b, s, _ = qkv.shape
    qw = SW_HEADS * HEAD_DIM
    tq = SW_SUB * BLOCK
    assert s % tq == 0 and SW_KV_HEADS * HEAD_DIM == LANES
    prev = lambda bi, n: (bi, jnp.maximum(n * SW_SUB - 1, 0))
    return pl.pallas_call(
        _sw_kernel,
        grid=(b, s // tq),
        in_specs=[
            pl.BlockSpec((1, tq, qw), lambda bi, n: (bi, n, q_col)),
            pl.BlockSpec((1, BLOCK, LANES), lambda bi, n: prev(bi, n) + (k_col,)),
            pl.BlockSpec((1, tq, LANES), lambda bi, n: (bi, n, k_col)),
            pl.BlockSpec((1, BLOCK, LANES), lambda bi, n: prev(bi, n) + (v_col,)),
            pl.BlockSpec((1, tq, LANES), lambda bi, n: (bi, n, v_col)),
            pl.BlockSpec((SW_KV_HEADS, SW_GROUP * BLOCK, 2 * BLOCK), lambda bi, n: (0, 0, 0)),
        ],
        out_specs=pl.BlockSpec((1, tq, qw), lambda bi, n: (bi, n, 0)),
        out_shape=jax.ShapeDtypeStruct((b, s, qw), _BF16),
        compiler_params=pltpu.CompilerParams(
            dimension_semantics=("arbitrary", "arbitrary"), vmem_limit_bytes=VMEM_LIMIT),
        name="sw_attn",
    )(qkv, qkv, qkv, qkv, qkv, table)


MIX_CHUNK = V7X_MXU_DIM


def _mix_kernel(x_ref, g_ref, wg_ref, ysb_ref, ysw_ref, wsb_ref, wsw_ref, wo_ref, o_ref,
                h_scr, merged_scr):
    d = x_ref.shape[1]
    tm = x_ref.shape[0] // ROW_SUB
    n_chunks = d // MIX_CHUNK
    rows = lambda s: slice(s * tm, (s + 1) * tm)

    def prologue(s):
        h_scr[rows(s), :] = _rms(x_ref[rows(s), :], g_ref[...]).astype(_BF16)

    def projections(s, c):
        cols = slice(c * MIX_CHUNK, (c + 1) * MIX_CHUNK)
        gate_cols = slice(d + c * MIX_CHUNK, d + (c + 1) * MIX_CHUNK)
        h = h_scr[rows(s), :]
        return (_dot(h, wg_ref[:, cols]), _dot(ysb_ref[rows(s), :], wsb_ref[:, cols]),
                _dot(h, wg_ref[:, gate_cols]), _dot(ysw_ref[rows(s), :], wsw_ref[:, cols]))

    def epilogue(s, y):
        o_ref[rows(s), :] = x_ref[rows(s), :] + y

    prologue(0)
    ahead = projections(0, 0)
    pending = None
    for s in range(ROW_SUB):
        for c in range(n_chunks):
            logit_sb, y_sb, logit_sw, y_sw = ahead
            if c + 1 < n_chunks:
                ahead = projections(s, c + 1)
            elif s + 1 < ROW_SUB:
                ahead = projections(s + 1, 0)
            merged = jax.nn.sigmoid(logit_sb) * y_sb + jax.nn.sigmoid(logit_sw) * y_sw
            merged_scr[rows(s), c * MIX_CHUNK:(c + 1) * MIX_CHUNK] = merged.astype(_BF16)
            if c == 0 and s + 1 < ROW_SUB:
                prologue(s + 1)
            if c == 1 and pending is not None:
                epilogue(*pending)
                pending = None
        pending = (s, _dot(merged_scr[rows(s), :], wo_ref[...]))
    epilogue(*pending)


def _mix_out(x2d, g, w_gate, y_sb, y_sw, w_sb, w_sw, w_out, *, tm):
    n, d = x2d.shape
    assert d % MIX_CHUNK == 0
    step_rows = ROW_SUB * tm
    full = lambda a: pl.BlockSpec(a.shape, lambda i: (0, 0), pipeline_mode=pl.Buffered(1))
    rows = lambda a: pl.BlockSpec((step_rows, a.shape[1]), lambda i: (i, 0))
    return pl.pallas_call(
        _mix_kernel,
        grid=(n // step_rows,),
        in_specs=[rows(x2d), full(g), full(w_gate), rows(y_sb), rows(y_sw),
                  full(w_sb), full(w_sw), full(w_out)],
        out_specs=pl.BlockSpec((step_rows, d), lambda i: (i, 0)),
        out_shape=jax.ShapeDtypeStruct((n, d), _F32),
        scratch_shapes=[pltpu.VMEM((step_rows, d), _BF16), pltpu.VMEM((step_rows, d), _BF16)],
        compiler_params=pltpu.CompilerParams(
            dimension_semantics=("arbitrary",), vmem_limit_bytes=VMEM_LIMIT),
        name="mix_out",
    )(x2d, g, w_gate, y_sb, y_sw, w_sb, w_sw, w_out)


FFN_SUB = 2
HALO = 8


def _ffn_kernel(x_ref, g_ref, wu_ref, cw_ref, cb_ref, wd_ref, gf_ref, o_ref,
                h_scr, act_scr, u_scr, carry_scr, *, steps_per_seq):
    tm = x_ref.shape[0] // FFN_SUB
    d_ff = wd_ref.shape[0]
    nj = d_ff // FF_CHUNK
    rows = lambda s: slice(s * tm, (s + 1) * tm)

    @pl.when(pl.program_id(0) % steps_per_seq == 0)
    def _():
        carry_scr[...] = jnp.zeros_like(carry_scr)

    def prologue(s):
        h_scr[rows(s), :] = _rms(x_ref[rows(s), :], g_ref[...]).astype(_BF16)

    def up_proj(s, j):
        h = h_scr[rows(s), :]
        gate = _dot(h, wu_ref[:, j * FF_CHUNK:(j + 1) * FF_CHUNK])
        value = _dot(h, wu_ref[:, d_ff + j * FF_CHUNK:d_ff + (j + 1) * FF_CHUNK])
        return gate, value

    def conv(slot, j, part, u):
        slabs = []
        for k in range(FF_CHUNK // LANES):
            slab = (slot * 2 + part) * (FF_CHUNK // LANES) + k
            cols = slice(part * d_ff + j * FF_CHUNK + k * LANES,
                         part * d_ff + j * FF_CHUNK + (k + 1) * LANES)
            lanes = slice((part * FF_CHUNK // LANES + k) * LANES,
                          (part * FF_CHUNK // LANES + k + 1) * LANES)
            uk = u[:, k * LANES:(k + 1) * LANES]
            u_scr[slab, HALO:HALO + tm, :] = uk
            u_scr[slab, 0:HALO, :] = carry_scr[j, :, lanes]
            carry_scr[j, :, lanes] = uk[tm - HALO:tm, :]
            y = cb_ref[:, cols] + cw_ref[CONV_WIDTH - 1:CONV_WIDTH, cols] * uk
            for tap in range(CONV_WIDTH - 1):
                back = CONV_WIDTH - 1 - tap
                y = y + cw_ref[tap:tap + 1, cols] * u_scr[slab, pl.ds(HALO - back, tm, stride=1), :]
            slabs.append(y)
        return jnp.concatenate(slabs, axis=1)

    def gated_conv(s, j, gate, value):
        slot = (s * nj + j) % 2
        act = jax.nn.silu(conv(slot, j, 0, gate)) * conv(slot, j, 1, value)
        act_scr[rows(s), j * FF_CHUNK:(j + 1) * FF_CHUNK] = act.astype(_BF16)

    def epilogue(s, y):
        o_ref[rows(s), :] = _rms(x_ref[rows(s), :] + y, gf_ref[...])

    prologue(0)
    ahead = up_proj(0, 0)
    pending = None
    for s in range(FFN_SUB):
        for j in range(nj):
            gate, value = ahead
            if j + 1 < nj:
                ahead = up_proj(s, j + 1)
            elif s + 1 < FFN_SUB:
                ahead = up_proj(s + 1, 0)
            gated_conv(s, j, gate, value)
            if j == 0 and s + 1 < FFN_SUB:
                prologue(s + 1)
            if j == 1 and pending is not None:
                epilogue(*pending)
                pending = None
        pending = (s, _dot(act_scr[rows(s), :], wd_ref[...]))
    epilogue(*pending)


def _conv_ffn(x1, g_ffn, w_up, conv_w, conv_b, w_down, g_final, *, tm, tiles_per_seq):
    n, d = x1.shape
    d_ff = w_down.shape[0]
    nj = d_ff // FF_CHUNK
    assert tiles_per_seq % FFN_SUB == 0
    step_rows = FFN_SUB * tm
    resident = lambda a: pl.BlockSpec(a.shape, lambda i: (0, 0), pipeline_mode=pl.Buffered(1))
    return pl.pallas_call(
        functools.partial(_ffn_kernel, steps_per_seq=tiles_per_seq // FFN_SUB),
        grid=(n // step_rows,),
        in_specs=[
            pl.BlockSpec((step_rows, d), lambda i: (i, 0)),
            resident(g_ffn), resident(w_up), resident(conv_w), resident(conv_b),
            resident(w_down), resident(g_final),
        ],
        out_specs=pl.BlockSpec((step_rows, d), lambda i: (i, 0)),
        out_shape=jax.ShapeDtypeStruct((n, d), _F32),
        scratch_shapes=[
            pltpu.VMEM((step_rows, d), _BF16),
            pltpu.VMEM((step_rows, d_ff), _BF16),
            pltpu.VMEM((2 * 2 * FF_CHUNK // LANES, HALO + tm, LANES), _F32),
            pltpu.VMEM((nj, HALO, 2 * FF_CHUNK), _F32),
        ],
        compiler_params=pltpu.CompilerParams(
            dimension_semantics=("arbitrary",), vmem_limit_bytes=VMEM_LIMIT),
        name="conv_ffn",
    )(x1, g_ffn, w_up, conv_w, conv_b, w_down, g_final)


def kernel(x, g_mix, w_in, w_sb_proj, w_sw_proj, w_out, rel_bias, sinks, g_ffn, w_up,
           conv_w, conv_b, w_down, g_final):
    b, s, d = x.shape
    n = b * s
    d_ff = w_down.shape[0]
    sb_w = SB_HEADS * HEAD_DIM
    sw_qw = SW_HEADS * HEAD_DIM
    sw_kw = SW_KV_HEADS * HEAD_DIM
    assert s % BLOCK == 0 and d_ff % FF_CHUNK == 0
    tm = min(ROW_TILE, s)
    assert s % (max(ROW_SUB, FFN_SUB) * tm) == 0

    qkv_end = 3 * sb_w + sw_qw
    order = np.arange(SW_HEADS).reshape(SW_KV_HEADS, SW_GROUP).T.reshape(-1)
    by_head = lambda a, axis: jnp.take(
        a.reshape(a.shape[:axis] + (SW_HEADS, HEAD_DIM) + a.shape[axis + 1:]), order,
        axis=axis).reshape(a.shape)
    w_in = w_in.astype(_BF16)
    w_qkv = jnp.concatenate(
        [w_in[:, :3 * sb_w], by_head(w_in[:, 3 * sb_w:qkv_end], 1),
         w_in[:, qkv_end:qkv_end + 2 * sw_kw]], axis=1)
    w_gate = w_in[:, qkv_end + 2 * sw_kw:]
    row = lambda a: a.reshape(1, -1).astype(_F32)

    x2d = x.reshape(n, d)
    assert sb_w == IN_CHUNK and sw_qw == IN_CHUNK
    qkv = _in_proj(x2d, row(g_mix), w_qkv, tm=tm, chunk=IN_CHUNK,
                   n_scaled=(0, 3 * sb_w // IN_CHUNK))
    qkv = qkv.reshape(b, s, -1)
    y_sb = _sb_attn(qkv, q_col=0, k_col=sb_w // LANES, v_col=2 * sb_w // LANES)
    y_sw = _sw_attn(qkv, _sw_bias(rel_bias, sinks), q_col=3 * sb_w // sw_qw,
                    k_col=qkv_end // LANES, v_col=qkv_end // LANES + 1)
    x1 = _mix_out(x2d, row(g_mix), w_gate, y_sb.reshape(n, sb_w), y_sw.reshape(n, sw_qw),
                  w_sb_proj.astype(_BF16), by_head(w_sw_proj, 0).astype(_BF16),
                  w_out.astype(_BF16), tm=tm)
    out = _conv_ffn(x1, row(g_ffn), w_up.astype(_BF16), conv_w.astype(_F32), row(conv_b),
                    w_down.astype(_BF16), row(g_final), tm=tm, tiles_per_seq=s // tm)
    return out.reshape(b, s, d)
```

```python
import functools
import math

import jax
import jax.numpy as jnp
import numpy as np
from jax import lax
from jax.experimental import pallas as pl
from jax.experimental.pallas import tpu as pltpu

SB_HEADS = 8
SW_HEADS = 8
SW_KV_HEADS = 2
HEAD_DIM = 64
BLOCK = 128
NUM_BUCKETS = 32
MAX_DISTANCE = 128
CONV_WIDTH = 3
EPS = 1e-6
MASKED = float("-inf")
LANES = 128
V7X_MXU_DIM = 256
V7X_VMEM_BYTES = 64 * 1024 * 1024
VMEM_LIMIT = V7X_VMEM_BYTES * 7 // 8
ROW_TILE = 512
IN_CHUNK = 2 * V7X_MXU_DIM
FF_CHUNK = V7X_MXU_DIM

_BF16 = jnp.bfloat16
_F32 = jnp.float32


def _rms(x, g):
    return x * lax.rsqrt(jnp.mean(x * x, axis=-1, keepdims=True) + EPS) * g


def _dot(a, b):
    return jnp.dot(a, b, preferred_element_type=_F32)


def _dot_nt(a, b):
    return lax.dot_general(a, b, (((1,), (1,)), ((), ())), preferred_element_type=_F32)


ROW_SUB = 2


def _in_proj_kernel(x_ref, g_ref, w_ref, o_ref, h_scr, *, n_scaled, chunk):
    tm = x_ref.shape[0] // ROW_SUB
    rows = lambda s: slice(s * tm, (s + 1) * tm)
    scale = 1.0 / math.sqrt(HEAD_DIM)

    def prologue(s):
        h_scr[rows(s), :] = _rms(x_ref[rows(s), :], g_ref[...]).astype(_BF16)

    prologue(0)
    width = w_ref.shape[1]
    for s in range(ROW_SUB):
        for c, start in enumerate(range(0, width, chunk)):
            cols = slice(start, min(start + chunk, width))
            p = _dot(h_scr[rows(s), :], w_ref[:, cols])
            if c in n_scaled:
                p = p * scale
            o_ref[rows(s), cols] = p.astype(_BF16)
            if c == 0 and s + 1 < ROW_SUB:
                prologue(s + 1)


def _in_proj(x2d, g, w, *, tm, n_scaled, chunk):
    n, d = x2d.shape
    width = w.shape[1]
    step_rows = ROW_SUB * tm
    return pl.pallas_call(
        functools.partial(_in_proj_kernel, n_scaled=n_scaled, chunk=chunk),
        grid=(n // step_rows,),
        in_specs=[
            pl.BlockSpec((step_rows, d), lambda i: (i, 0)),
            pl.BlockSpec((1, d), lambda i: (0, 0), pipeline_mode=pl.Buffered(1)),
            pl.BlockSpec((d, width), lambda i: (0, 0), pipeline_mode=pl.Buffered(1)),
        ],
        out_specs=pl.BlockSpec((step_rows, width), lambda i: (i, 0)),
        out_shape=jax.ShapeDtypeStruct((n, width), _BF16),
        scratch_shapes=[pltpu.VMEM((step_rows, d), _BF16)],
        compiler_params=pltpu.CompilerParams(
            dimension_semantics=("arbitrary",), vmem_limit_bytes=VMEM_LIMIT),
        name="in_proj",
    )(x2d, g, w)


SB_SUB = 32
SB_TOP = 32
SB_MASKED = -1e30
SB_SKEW = 3
SB_OFF = 1e30
LOG2E = 1.4426950408889634
SB_DEAD_LOG2 = 151.0


def _sb_kernel(q_ref, k_ref, v_ref, t_ref, o_ref, rem_scr, acc_scr):
    base = pl.program_id(2) * SB_SUB
    tri = t_ref[...]
    rows2, top2 = 2 * BLOCK, 2 * SB_TOP
    rest = BLOCK - SB_TOP
    lane = lax.broadcasted_iota(jnp.int32, (rows2, LANES), 1)
    row = lax.broadcasted_iota(jnp.int32, (rows2, LANES), 0)
    head1 = ((row >= SB_TOP) & (row < top2)) | (row >= top2 + rest)
    own_lanes = (lane >= HEAD_DIM) == head1
    query = jnp.where(row < SB_TOP, row, jnp.where(row < top2 + rest, row - SB_TOP, row - BLOCK))
    strictly_earlier = lane < query

    def stacked_q(a):
        q = q_ref[0, a * BLOCK:(a + 1) * BLOCK, :]
        q2 = jnp.concatenate([q[:SB_TOP], q[:SB_TOP], q[SB_TOP:], q[SB_TOP:]], axis=0)
        return jnp.where(own_lanes, q2, jnp.zeros_like(q2))

    def key_rows(kb):
        return pl.ds(pl.multiple_of(kb * BLOCK, BLOCK), BLOCK)

    def scores(q2, kb, diagonal):
        z = _dot_nt(q2, k_ref[0, key_rows(kb), :]) * LOG2E
        return jnp.where(strictly_earlier, z, SB_MASKED) if diagonal else z

    def suffix_mass(z):
        sign = jnp.uint32(0x80000000)
        neg_abs = lax.bitcast_convert_type(lax.bitcast_convert_type(z, jnp.uint32) | sign, _F32)
        sp = jnp.maximum(z, 0.0) + jnp.log2(1.0 + jnp.exp2(neg_abs))
        hi = sp.astype(_BF16)
        lo = (sp - hi.astype(_F32)).astype(_BF16)
        cum = _dot(jnp.concatenate([hi, lo], axis=1), tri)
        return cum, jnp.broadcast_to(cum[:, 0:1], cum.shape)

    def weights(z, cum, rem):
        arg = z - cum
        return jnp.exp2(arg if rem is None else arg - rem).astype(_BF16)

    def weighted_values(z, cum, rem, kb):
        return _dot(weights(z, cum, rem), v_ref[0, key_rows(kb), :])

    def earlier_block(a, t, rem):
        kb = base + a - t
        return jnp.maximum(kb, 0), rem + jnp.where(kb < 0, SB_OFF, 0.0)

    items = [(a, t, slice(0, top2) if t == 2 else slice(0, rows2))
             for a in range(SB_SUB) for t in range(3)]
    q2s, zs, cums = {}, {}, {}
    rems, accs = [None] * SB_SUB, [None] * SB_SUB
    for step in range(len(items) + 2 * SB_SKEW):
        if step < len(items):
            a, t, part = items[step]
            if t == 0:
                q2s[a] = stacked_q(a)
            zs[step] = scores(q2s[a][part], jnp.maximum(base + a - t, 0), t == 0)
        i = step - SB_SKEW
        if 0 <= i < len(items):
            cums[i] = suffix_mass(zs[i])
        i = step - 2 * SB_SKEW
        if 0 <= i < len(items):
            a, t, part = items[i]
            if t == 0:
                kb, rem = base + a, None
            elif a >= t:
                kb, rem = base + a - t, rems[a][part]
            else:
                kb, rem = earlier_block(a, t, rems[a][part])
            cum, total = cums.pop(i)
            if a >= 1 and t == 0:
                acc = weights(zs.pop(i), cum, rem)
            elif a >= 1 and t == 1:
                w2 = jnp.concatenate([weights(zs.pop(i), cum, rem), accs[a]], axis=1)
                acc = _dot(w2, v_ref[0, pl.ds(pl.multiple_of(kb * BLOCK, BLOCK), 2 * BLOCK), :])
            else:
                acc = weighted_values(zs.pop(i), cum, rem, kb)
                if t > 0:
                    acc = accs[a][part] + acc
            rem = rem + total if t > 0 else total
            if part.stop < rows2:
                rem = jnp.concatenate([rem, rems[a][part.stop:]], axis=0)
                acc = jnp.concatenate([acc, accs[a][part.stop:]], axis=0)
            rems[a], accs[a] = rem, acc
    floor = None
    for a in range(SB_SUB):
        rem_scr[a] = rems[a]
        acc_scr[a] = accs[a]
        floor = rems[a] if floor is None else jnp.minimum(floor, rems[a])

    def cond(carry):
        t, least = carry
        return (base + SB_SUB - 1 - t >= 0) & (least < SB_DEAD_LOG2)

    def body(carry):
        t, _ = carry
        floor = None
        for a in range(SB_SUB):
            q2 = stacked_q(a)
            for part, back in ((slice(top2, rows2), t), (slice(0, top2), t + 1)):
                kb, rem = earlier_block(a, back, rem_scr[a, part, :])
                z = scores(q2[part], kb, False)
                cum, total = suffix_mass(z)
                acc_scr[a, part, :] += weighted_values(z, cum, rem, kb)
                rem = rem + total
                rem_scr[a, part, :] = rem
                least = jnp.min(rem, axis=0, keepdims=True)
                floor = least if floor is None else jnp.minimum(floor, least)
        return t + 1, jnp.min(floor)

    lax.while_loop(cond, body, (jnp.int32(2), jnp.min(floor)))
    head0_lanes = lax.broadcasted_iota(jnp.int32, (BLOCK, LANES), 1) < HEAD_DIM
    for a in range(SB_SUB):
        head0 = jnp.concatenate([acc_scr[a, :SB_TOP, :], acc_scr[a, top2:top2 + rest, :]], axis=0)
        head1 = jnp.concatenate([acc_scr[a, SB_TOP:top2, :], acc_scr[a, top2 + rest:, :]], axis=0)
        o_ref[0, a * BLOCK:(a + 1) * BLOCK, :] = jnp.where(
            head0_lanes, head0, head1).astype(o_ref.dtype)


def _suffix_sum_matrix():
    j = np.arange(2 * BLOCK)[:, None] % BLOCK
    s = np.arange(BLOCK)[None, :]
    return jnp.asarray((j >= s).astype(np.float32), dtype=_BF16)


def _sb_attn(qkv, *, q_col, k_col, v_col):
    b, s, _ = qkv.shape
    pairs = SB_HEADS * HEAD_DIM // LANES
    tq = SB_SUB * BLOCK
    assert s % tq == 0
    return pl.pallas_call(
        _sb_kernel,
        grid=(b, pairs, s // tq),
        in_specs=[
            pl.BlockSpec((1, tq, LANES), lambda bi, p, qi: (bi, qi, q_col + p)),
            pl.BlockSpec((1, s, LANES), lambda bi, p, qi: (bi, 0, k_col + p)),
            pl.BlockSpec((1, s, LANES), lambda bi, p, qi: (bi, 0, v_col + p)),
            pl.BlockSpec((2 * BLOCK, BLOCK), lambda bi, p, qi: (0, 0)),
        ],
        out_specs=pl.BlockSpec((1, tq, LANES), lambda bi, p, qi: (bi, qi, p)),
        out_shape=jax.ShapeDtypeStruct((b, s, SB_HEADS * HEAD_DIM), _BF16),
        scratch_shapes=[
            pltpu.VMEM((SB_SUB, 2 * BLOCK, BLOCK), _F32),
            pltpu.VMEM((SB_SUB, 2 * BLOCK, LANES), _F32),
        ],
        compiler_params=pltpu.CompilerParams(
            dimension_semantics=("arbitrary", "arbitrary", "arbitrary"),
            vmem_limit_bytes=VMEM_LIMIT),
        name="sb_attn",
    )(qkv, qkv, qkv, _suffix_sum_matrix())


def _sw_bias_kernel(bucket_ref, rel_ref, sink_ref, o_ref):
    bucket = bucket_ref[...]
    col = lax.broadcasted_iota(jnp.int32, bucket.shape, 1)
    dist = lax.broadcasted_iota(jnp.int32, bucket.shape, 0) + BLOCK - col
    in_window = (dist >= 0) & (dist < BLOCK)
    for h in range(SW_HEADS):
        acc = jnp.zeros(bucket.shape, _F32)
        for b in range(NUM_BUCKETS):
            acc = jnp.where(bucket == b, rel_ref[b, h], acc)
        o_ref[h // SW_GROUP, (h % SW_GROUP) * BLOCK:(h % SW_GROUP + 1) * BLOCK, :] = jnp.where(
            col == 0, sink_ref[h], jnp.where(in_window, acc, MASKED))


def _t5_bucket(dist):
    max_exact = NUM_BUCKETS // 2
    d = np.maximum(dist, 1).astype(np.float32)
    large = max_exact + (np.log(d / max_exact) / math.log(MAX_DISTANCE / max_exact)
                         * (NUM_BUCKETS - max_exact)).astype(np.int32)
    large = np.minimum(large, NUM_BUCKETS - 1)
    return np.where(dist < max_exact, dist, large).astype(np.int32)


def _sw_bias(rel_bias, sinks):
    dist = (np.arange(BLOCK)[:, None] + BLOCK) - np.arange(2 * BLOCK)[None, :]
    bucket = jnp.asarray(_t5_bucket(np.maximum(dist, 0)))
    return pl.pallas_call(
        _sw_bias_kernel,
        in_specs=[
            pl.BlockSpec(memory_space=pltpu.VMEM),
            pl.BlockSpec(memory_space=pltpu.SMEM),
            pl.BlockSpec(memory_space=pltpu.SMEM),
        ],
        out_specs=pl.BlockSpec(memory_space=pltpu.VMEM),
        out_shape=jax.ShapeDtypeStruct((SW_KV_HEADS, SW_GROUP * BLOCK, 2 * BLOCK), _F32),
        name="sw_bias",
    )(bucket, rel_bias.astype(_F32), sinks.astype(_F32))


SW_SUB = 32
SW_GROUP = SW_HEADS // SW_KV_HEADS


def _sw_kernel(q_ref, kp_ref, kc_ref, vp_ref, vc_ref, bias_ref, o_ref):
    rows = SW_GROUP * BLOCK
    upper_lanes = lax.broadcasted_iota(jnp.int32, (rows, LANES), 1) >= HEAD_DIM
    kv_upper = lax.broadcasted_iota(jnp.int32, (2 * BLOCK, LANES), 1) >= HEAD_DIM
    not_sink = lax.broadcasted_iota(jnp.int32, (2 * BLOCK, LANES), 0) > 0
    kpos = lax.broadcasted_iota(jnp.int32, (rows, 2 * BLOCK), 1)
    first_ok = (kpos >= jnp.where(pl.program_id(1) > 0, 0, BLOCK)) | (kpos == 0)

    def kv_rows(prev_ref, cur_ref, blk, keep):
        cur = cur_ref[0, blk * BLOCK:(blk + 1) * BLOCK, :]
        prev = prev_ref[0] if blk == 0 else cur_ref[0, (blk - 1) * BLOCK:blk * BLOCK, :]
        kv = jnp.concatenate([prev, cur], axis=0)
        return jnp.where(keep, kv, jnp.zeros_like(kv))

    def logits(blk, g):
        q4 = jnp.concatenate(
            [q_ref[0, blk * BLOCK:(blk + 1) * BLOCK, t * LANES:(t + 1) * LANES]
             for t in range(SW_GROUP)], axis=0)
        keys = kv_rows(kp_ref, kc_ref, blk, not_sink & (kv_upper if g == 1 else ~kv_upper))
        return _dot_nt(q4, keys)

    def exponentials(z, blk, g):
        z = z + bias_ref[g]
        if blk == 0:
            z = jnp.where(first_ok, z, MASKED)
        return jnp.exp(z - jnp.max(z, axis=-1, keepdims=True)).astype(_BF16)

    def values(p, blk):
        vv = kv_rows(vp_ref, vc_ref, blk, not_sink)
        pv = _dot(p, jnp.concatenate([vv, jnp.ones_like(vv)], axis=1))
        return pv[:, :LANES] * (1.0 / pv[:, LANES:])

    items = [(blk, g) for blk in range(SW_SUB) for g in range(SW_KV_HEADS)]
    zs, ps, outs = {}, {}, {}
    for step in range(len(items) + 2):
        if step < len(items):
            zs[step] = logits(*items[step])
        if 0 <= step - 1 < len(items):
            ps[step - 1] = exponentials(zs.pop(step - 1), *items[step - 1])
        if 0 <= step - 2 < len(items):
            blk, g = items[step - 2]
            outs[g] = values(ps.pop(step - 2), blk)
            if g == SW_KV_HEADS - 1:
                y = jnp.where(upper_lanes, outs.pop(1), outs.pop(0)).astype(o_ref.dtype)
                for t in range(SW_GROUP):
                    o_ref[0, blk * BLOCK:(blk + 1) * BLOCK, t * LANES:(t + 1) * LANES] = (
                        y[t * BLOCK:(t + 1) * BLOCK])


def _sw_attn(qkv, table, *, q_col, k_col, v_col):
    b, s, _ = qkv.shape
    qw = SW_HEADS * HEAD_DIM
    tq = SW_SUB * BLOCK
    assert s % tq == 0 and SW_KV_HEADS * HEAD_DIM == LANES
    prev = lambda bi, n: (bi, jnp.maximum(n * SW_SUB - 1, 0))
    return pl.pallas_call(
        _sw_kernel,
        grid=(b, s // tq),
        in_specs=[
            pl.BlockSpec((1, tq, qw), lambda bi, n: (bi, n, q_col)),
            pl.BlockSpec((1, BLOCK, LANES), lambda bi, n: prev(bi, n) + (k_col,)),
            pl.BlockSpec((1, tq, LANES), lambda bi, n: (bi, n, k_col)),
            pl.BlockSpec((1, BLOCK, LANES), lambda bi, n: prev(bi, n) + (v_col,)),
            pl.BlockSpec((1, tq, LANES), lambda bi, n: (bi, n, v_col)),
            pl.BlockSpec((SW_KV_HEADS, SW_GROUP * BLOCK, 2 * BLOCK), lambda bi, n: (0, 0, 0)),
        ],
        out_specs=pl.BlockSpec((1, tq, qw), lambda bi, n: (bi, n, 0)),
        out_shape=jax.ShapeDtypeStruct((b, s, qw), _BF16),
        compiler_params=pltpu.CompilerParams(
            dimension_semantics=("arbitrary", "arbitrary"), vmem_limit_bytes=VMEM_LIMIT),
        name="sw_attn",
    )(qkv, qkv, qkv, qkv, qkv, table)


MIX_CHUNK = V7X_MXU_DIM


def _mix_kernel(x_ref, g_ref, wg_ref, ysb_ref, ysw_ref, wsb_ref, wsw_ref, wo_ref, o_ref,
                h_scr, merged_scr):
    d = x_ref.shape[1]
    tm = x_ref.shape[0] // ROW_SUB
    n_chunks = d // MIX_CHUNK
    rows = lambda s: slice(s * tm, (s + 1) * tm)

    def prologue(s):
        h_scr[rows(s), :] = _rms(x_ref[rows(s), :], g_ref[...]).astype(_BF16)

    def projections(s, c):
        cols = slice(c * MIX_CHUNK, (c + 1) * MIX_CHUNK)
        gate_cols = slice(d + c * MIX_CHUNK, d + (c + 1) * MIX_CHUNK)
        h = h_scr[rows(s), :]
        return (_dot(h, wg_ref[:, cols]), _dot(ysb_ref[rows(s), :], wsb_ref[:, cols]),
                _dot(h, wg_ref[:, gate_cols]), _dot(ysw_ref[rows(s), :], wsw_ref[:, cols]))

    def epilogue(s, y):
        o_ref[rows(s), :] = x_ref[rows(s), :] + y

    prologue(0)
    ahead = projections(0, 0)
    pending = None
    for s in range(ROW_SUB):
        for c in range(n_chunks):
            logit_sb, y_sb, logit_sw, y_sw = ahead
            if c + 1 < n_chunks:
                ahead = projections(s, c + 1)
            elif s + 1 < ROW_SUB:
                ahead = projections(s + 1, 0)
            merged = jax.nn.sigmoid(logit_sb) * y_sb + jax.nn.sigmoid(logit_sw) * y_sw
            merged_scr[rows(s), c * MIX_CHUNK:(c + 1) * MIX_CHUNK] = merged.astype(_BF16)
            if c == 0 and s + 1 < ROW_SUB:
                prologue(s + 1)
            if c == 1 and pending is not None:
                epilogue(*pending)
                pending = None
        pending = (s, _dot(merged_scr[rows(s), :], wo_ref[...]))
    epilogue(*pending)


def _mix_out(x2d, g, w_gate, y_sb, y_sw, w_sb, w_sw, w_out, *, tm):
    n, d = x2d.shape
    assert d % MIX_CHUNK == 0
    step_rows = ROW_SUB * tm
    full = lambda a: pl.BlockSpec(a.shape, lambda i: (0, 0), pipeline_mode=pl.Buffered(1))
    rows = lambda a: pl.BlockSpec((step_rows, a.shape[1]), lambda i: (i, 0))
    return pl.pallas_call(
        _mix_kernel,
        grid=(n // step_rows,),
        in_specs=[rows(x2d), full(g), full(w_gate), rows(y_sb), rows(y_sw),
                  full(w_sb), full(w_sw), full(w_out)],
        out_specs=pl.BlockSpec((step_rows, d), lambda i: (i, 0)),
        out_shape=jax.ShapeDtypeStruct((n, d), _F32),
        scratch_shapes=[pltpu.VMEM((step_rows, d), _BF16), pltpu.VMEM((step_rows, d), _BF16)],
        compiler_params=pltpu.CompilerParams(
            dimension_semantics=("arbitrary",), vmem_limit_bytes=VMEM_LIMIT),
        name="mix_out",
    )(x2d, g, w_gate, y_sb, y_sw, w_sb, w_sw, w_out)


FFN_SUB = 2
HALO = 8


def _ffn_kernel(x_ref, g_ref, wu_ref, cw_ref, cb_ref, wd_ref, gf_ref, o_ref,
                h_scr, act_scr, u_scr, carry_scr, *, steps_per_seq):
    tm = x_ref.shape[0] // FFN_SUB
    d_ff = wd_ref.shape[0]
    nj = d_ff // FF_CHUNK
    rows = lambda s: slice(s * tm, (s + 1) * tm)

    @pl.when(pl.program_id(0) % steps_per_seq == 0)
    def _():
        carry_scr[...] = jnp.zeros_like(carry_scr)

    def prologue(s):
        h_scr[rows(s), :] = _rms(x_ref[rows(s), :], g_ref[...]).astype(_BF16)

    def up_proj(s, j):
        h = h_scr[rows(s), :]
        gate = _dot(h, wu_ref[:, j * FF_CHUNK:(j + 1) * FF_CHUNK])
        value = _dot(h, wu_ref[:, d_ff + j * FF_CHUNK:d_ff + (j + 1) * FF_CHUNK])
        return gate, value

    def conv(slot, j, part, u):
        slabs = []
        for k in range(FF_CHUNK // LANES):
            slab = (slot * 2 + part) * (FF_CHUNK // LANES) + k
            cols = slice(part * d_ff + j * FF_CHUNK + k * LANES,
                         part * d_ff + j * FF_CHUNK + (k + 1) * LANES)
            lanes = slice((part * FF_CHUNK // LANES + k) * LANES,
                          (part * FF_CHUNK // LANES + k + 1) * LANES)
            uk = u[:, k * LANES:(k + 1) * LANES]
            u_scr[slab, HALO:HALO + tm, :] = uk
            u_scr[slab, 0:HALO, :] = carry_scr[j, :, lanes]
            carry_scr[j, :, lanes] = uk[tm - HALO:tm, :]
            y = cb_ref[:, cols] + cw_ref[CONV_WIDTH - 1:CONV_WIDTH, cols] * uk
            for tap in range(CONV_WIDTH - 1):
                back = CONV_WIDTH - 1 - tap
                y = y + cw_ref[tap:tap + 1, cols] * u_scr[slab, pl.ds(HALO - back, tm, stride=1), :]
            slabs.append(y)
        return jnp.concatenate(slabs, axis=1)

    def gated_conv(s, j, gate, value):
        slot = (s * nj + j) % 2
        act = jax.nn.silu(conv(slot, j, 0, gate)) * conv(slot, j, 1, value)
        act_scr[rows(s), j * FF_CHUNK:(j + 1) * FF_CHUNK] = act.astype(_BF16)

    def epilogue(s, y):
        o_ref[rows(s), :] = _rms(x_ref[rows(s), :] + y, gf_ref[...])

    prologue(0)
    ahead = up_proj(0, 0)
    pending = None
    for s in range(FFN_SUB):
        for j in range(nj):
            gate, value = ahead
            if j + 1 < nj:
                ahead = up_proj(s, j + 1)
            elif s + 1 < FFN_SUB:
                ahead = up_proj(s + 1, 0)
            gated_conv(s, j, gate, value)
            if j == 0 and s + 1 < FFN_SUB:
                prologue(s + 1)
            if j == 1 and pending is not None:
                epilogue(*pending)
                pending = None
        pending = (s, _dot(act_scr[rows(s), :], wd_ref[...]))
    epilogue(*pending)


def _conv_ffn(x1, g_ffn, w_up, conv_w, conv_b, w_down, g_final, *, tm, tiles_per_seq):
    n, d = x1.shape
    d_ff = w_down.shape[0]
    nj = d_ff // FF_CHUNK
    assert tiles_per_seq % FFN_SUB == 0
    step_rows = FFN_SUB * tm
    resident = lambda a: pl.BlockSpec(a.shape, lambda i: (0, 0), pipeline_mode=pl.Buffered(1))
    return pl.pallas_call(
        functools.partial(_ffn_kernel, steps_per_seq=tiles_per_seq // FFN_SUB),
        grid=(n // step_rows,),
        in_specs=[
            pl.BlockSpec((step_rows, d), lambda i: (i, 0)),
            resident(g_ffn), resident(w_up), resident(conv_w), resident(conv_b),
            resident(w_down), resident(g_final),
        ],
        out_specs=pl.BlockSpec((step_rows, d), lambda i: (i, 0)),
        out_shape=jax.ShapeDtypeStruct((n, d), _F32),
        scratch_shapes=[
            pltpu.VMEM((step_rows, d), _BF16),
            pltpu.VMEM((step_rows, d_ff), _BF16),
            pltpu.VMEM((2 * 2 * FF_CHUNK // LANES, HALO + tm, LANES), _F32),
            pltpu.VMEM((nj, HALO, 2 * FF_CHUNK), _F32),
        ],
        compiler_params=pltpu.CompilerParams(
            dimension_semantics=("arbitrary",), vmem_limit_bytes=VMEM_LIMIT),
        name="conv_ffn",
    )(x1, g_ffn, w_up, conv_w, conv_b, w_down, g_final)


def kernel(x, g_mix, w_in, w_sb_proj, w_sw_proj, w_out, rel_bias, sinks, g_ffn, w_up,
           conv_w, conv_b, w_down, g_final):
    b, s, d = x.shape
    n = b * s
    d_ff = w_down.shape[0]
    sb_w = SB_HEADS * HEAD_DIM
    sw_qw = SW_HEADS * HEAD_DIM
    sw_kw = SW_KV_HEADS * HEAD_DIM
    assert s % BLOCK == 0 and d_ff % FF_CHUNK == 0
    tm = min(ROW_TILE, s)
    assert s % (max(ROW_SUB, FFN_SUB) * tm) == 0

    qkv_end = 3 * sb_w + sw_qw
    order = np.arange(SW_HEADS).reshape(SW_KV_HEADS, SW_GROUP).T.reshape(-1)
    by_head = lambda a, axis: jnp.take(
        a.reshape(a.shape[:axis] + (SW_HEADS, HEAD_DIM) + a.shape[axis + 1:]), order,
        axis=axis).reshape(a.shape)
    w_in = w_in.astype(_BF16)
    w_qkv = jnp.concatenate(
        [w_in[:, :3 * sb_w], by_head(w_in[:, 3 * sb_w:qkv_end], 1),
         w_in[:, qkv_end:qkv_end + 2 * sw_kw]], axis=1)
    w_gate = w_in[:, qkv_end + 2 * sw_kw:]
    row = lambda a: a.reshape(1, -1).astype(_F32)

    x2d = x.reshape(n, d)
    assert sb_w == IN_CHUNK and sw_qw == IN_CHUNK
    qkv = _in_proj(x2d, row(g_mix), w_qkv, tm=tm, chunk=IN_CHUNK,
                   n_scaled=(0, 3 * sb_w // IN_CHUNK))
    qkv = qkv.reshape(b, s, -1)
    y_sb = _sb_attn(qkv, q_col=0, k_col=sb_w // LANES, v_col=2 * sb_w // LANES)
    y_sw = _sw_attn(qkv, _sw_bias(rel_bias, sinks), q_col=3 * sb_w // sw_qw,
                    k_col=qkv_end // LANES, v_col=qkv_end // LANES + 1)
    x1 = _mix_out(x2d, row(g_mix), w_gate, y_sb.reshape(n, sb_w), y_sw.reshape(n, sw_qw),
                  w_sb_proj.astype(_BF16), by_head(w_sw_proj, 0).astype(_BF16),
                  w_out.astype(_BF16), tm=tm)
    out = _conv_ffn(x1, row(g_ffn), w_up.astype(_BF16), conv_w.astype(_F32), row(conv_b),
                    w_down.astype(_BF16), row(g_final), tm=tm, tiles_per_seq=s // tm)
    return out.reshape(b, s, d)
```

```python
import functools
import math

import jax
import jax.numpy as jnp
import numpy as np
from jax import lax
from jax.experimental import pallas as pl
from jax.experimental.pallas import tpu as pltpu

SB_HEADS = 8
SW_HEADS = 8
SW_KV_HEADS = 2
HEAD_DIM = 64
BLOCK = 128
NUM_BUCKETS = 32
MAX_DISTANCE = 128
CONV_WIDTH = 3
EPS = 1e-6
MASKED = float("-inf")
LANES = 128
V7X_MXU_DIM = 256
V7X_VMEM_BYTES = 64 * 1024 * 1024
VMEM_LIMIT = V7X_VMEM_BYTES * 7 // 8
ROW_TILE = 512
IN_CHUNK = 2 * V7X_MXU_DIM
FF_CHUNK = V7X_MXU_DIM

_BF16 = jnp.bfloat16
_F32 = jnp.float32


def _rms(x, g):
    return x * lax.rsqrt(jnp.mean(x * x, axis=-1, keepdims=True) + EPS) * g


def _dot(a, b):
    return jnp.dot(a, b, preferred_element_type=_F32)


def _dot_nt(a, b):
    return lax.dot_general(a, b, (((1,), (1,)), ((), ())), preferred_element_type=_F32)


ROW_SUB = 2


def _in_proj_kernel(x_ref, g_ref, w_ref, o_ref, h_ref, *, n_scaled, chunk):
    tm = x_ref.shape[0] // ROW_SUB
    rows = lambda s: slice(s * tm, (s + 1) * tm)
    scale = 1.0 / math.sqrt(HEAD_DIM)

    def prologue(s):
        h_ref[rows(s), :] = _rms(x_ref[rows(s), :], g_ref[...]).astype(_BF16)

    prologue(0)
    width = w_ref.shape[1]
    for s in range(ROW_SUB):
        for c, start in enumerate(range(0, width, chunk)):
            cols = slice(start, min(start + chunk, width))
            p = _dot(h_ref[rows(s), :], w_ref[:, cols])
            if c in n_scaled:
                p = p * scale
            o_ref[rows(s), cols] = p.astype(_BF16)
            if c == 0 and s + 1 < ROW_SUB:
                prologue(s + 1)


def _in_proj(x2d, g, w, *, tm, n_scaled, chunk):
    n, d = x2d.shape
    width = w.shape[1]
    step_rows = ROW_SUB * tm
    return pl.pallas_call(
        functools.partial(_in_proj_kernel, n_scaled=n_scaled, chunk=chunk),
        grid=(n // step_rows,),
        in_specs=[
            pl.BlockSpec((step_rows, d), lambda i: (i, 0)),
            pl.BlockSpec((1, d), lambda i: (0, 0), pipeline_mode=pl.Buffered(1)),
            pl.BlockSpec((d, width), lambda i: (0, 0), pipeline_mode=pl.Buffered(1)),
        ],
        out_specs=[pl.BlockSpec((step_rows, width), lambda i: (i, 0)),
                   pl.BlockSpec((step_rows, d), lambda i: (i, 0))],
        out_shape=[jax.ShapeDtypeStruct((n, width), _BF16), jax.ShapeDtypeStruct((n, d), _BF16)],
        compiler_params=pltpu.CompilerParams(
            dimension_semantics=("arbitrary",), vmem_limit_bytes=VMEM_LIMIT),
        name="in_proj",
    )(x2d, g, w)


SB_SUB = 32
SB_TOP = 32
SB_MASKED = -1e30
SB_SKEW = 3
SB_OFF = 1e30
LOG2E = 1.4426950408889634
SB_DEAD_LOG2 = 151.0


def _sb_kernel(q_ref, k_ref, v_ref, t_ref, o_ref, rem_scr, acc_scr):
    base = pl.program_id(2) * SB_SUB
    tri = t_ref[...]
    rows2, top2 = 2 * BLOCK, 2 * SB_TOP
    rest = BLOCK - SB_TOP
    lane = lax.broadcasted_iota(jnp.int32, (rows2, LANES), 1)
    row = lax.broadcasted_iota(jnp.int32, (rows2, LANES), 0)
    head1 = ((row >= SB_TOP) & (row < top2)) | (row >= top2 + rest)
    own_lanes = (lane >= HEAD_DIM) == head1
    query = jnp.where(row < SB_TOP, row, jnp.where(row < top2 + rest, row - SB_TOP, row - BLOCK))
    strictly_earlier = lane < query

    def stacked_q(a):
        q = q_ref[0, a * BLOCK:(a + 1) * BLOCK, :]
        q2 = jnp.concatenate([q[:SB_TOP], q[:SB_TOP], q[SB_TOP:], q[SB_TOP:]], axis=0)
        return jnp.where(own_lanes, q2, jnp.zeros_like(q2))

    def key_rows(kb):
        return pl.ds(pl.multiple_of(kb * BLOCK, BLOCK), BLOCK)

    def scores(q2, kb, diagonal):
        z = _dot_nt(q2, k_ref[0, key_rows(kb), :]) * LOG2E
        return jnp.where(strictly_earlier, z, SB_MASKED) if diagonal else z

    def suffix_mass(z):
        sign = jnp.uint32(0x80000000)
        neg_abs = lax.bitcast_convert_type(lax.bitcast_convert_type(z, jnp.uint32) | sign, _F32)
        sp = jnp.maximum(z, 0.0) + jnp.log2(1.0 + jnp.exp2(neg_abs))
        hi = sp.astype(_BF16)
        lo = (sp - hi.astype(_F32)).astype(_BF16)
        cum = _dot(jnp.concatenate([hi, lo], axis=1), tri)
        return cum, jnp.broadcast_to(cum[:, 0:1], cum.shape)

    def weighted_values(z, cum, rem, kb):
        arg = z - cum
        w = jnp.exp2(arg if rem is None else arg - rem)
        return _dot(w.astype(_BF16), v_ref[0, key_rows(kb), :])

    def earlier_block(a, t, rem):
        kb = base + a - t
        return jnp.maximum(kb, 0), rem + jnp.where(kb < 0, SB_OFF, 0.0)

    items = [(a, t, slice(0, top2) if t == 2 else slice(0, rows2))
             for a in range(SB_SUB) for t in range(3)]
    q2s, zs, cums = {}, {}, {}
    rems, accs = [None] * SB_SUB, [None] * SB_SUB
    for step in range(len(items) + 2 * SB_SKEW):
        if step < len(items):
            a, t, part = items[step]
            if t == 0:
                q2s[a] = stacked_q(a)
            zs[step] = scores(q2s[a][part], jnp.maximum(base + a - t, 0), t == 0)
        i = step - SB_SKEW
        if 0 <= i < len(items):
            cums[i] = suffix_mass(zs[i])
        i = step - 2 * SB_SKEW
        if 0 <= i < len(items):
            a, t, part = items[i]
            if t == 0:
                kb, rem = base + a, None
            elif a >= t:
                kb, rem = base + a - t, rems[a][part]
            else:
                kb, rem = earlier_block(a, t, rems[a][part])
            cum, total = cums.pop(i)
            acc = weighted_values(zs.pop(i), cum, rem, kb)
            if t > 0:
                rem, acc = rem + total, accs[a][part] + acc
            else:
                rem = total
            if part.stop < rows2:
                rem = jnp.concatenate([rem, rems[a][part.stop:]], axis=0)
                acc = jnp.concatenate([acc, accs[a][part.stop:]], axis=0)
            rems[a], accs[a] = rem, acc
    floor = None
    for a in range(SB_SUB):
        rem_scr[a] = rems[a]
        acc_scr[a] = accs[a]
        floor = rems[a] if floor is None else jnp.minimum(floor, rems[a])

    def cond(carry):
        t, least = carry
        return (base + SB_SUB - 1 - t >= 0) & (least < SB_DEAD_LOG2)

    def body(carry):
        t, _ = carry
        floor = None
        for a in range(SB_SUB):
            q2 = stacked_q(a)
            for part, back in ((slice(top2, rows2), t), (slice(0, top2), t + 1)):
                kb, rem = earlier_block(a, back, rem_scr[a, part, :])
                z = scores(q2[part], kb, False)
                cum, total = suffix_mass(z)
                acc_scr[a, part, :] += weighted_values(z, cum, rem, kb)
                rem = rem + total
                rem_scr[a, part, :] = rem
                least = jnp.min(rem, axis=0, keepdims=True)
                floor = least if floor is None else jnp.minimum(floor, least)
        return t + 1, jnp.min(floor)

    lax.while_loop(cond, body, (jnp.int32(2), jnp.min(floor)))
    head0_lanes = lax.broadcasted_iota(jnp.int32, (BLOCK, LANES), 1) < HEAD_DIM
    for a in range(SB_SUB):
        head0 = jnp.concatenate([acc_scr[a, :SB_TOP, :], acc_scr[a, top2:top2 + rest, :]], axis=0)
        head1 = jnp.concatenate([acc_scr[a, SB_TOP:top2, :], acc_scr[a, top2 + rest:, :]], axis=0)
        o_ref[0, a * BLOCK:(a + 1) * BLOCK, :] = jnp.where(
            head0_lanes, head0, head1).astype(o_ref.dtype)


def _suffix_sum_matrix():
    j = np.arange(2 * BLOCK)[:, None] % BLOCK
    s = np.arange(BLOCK)[None, :]
    return jnp.asarray((j >= s).astype(np.float32), dtype=_BF16)


def _sb_attn(qkv, *, q_col, k_col, v_col):
    b, s, _ = qkv.shape
    pairs = SB_HEADS * HEAD_DIM // LANES
    tq = SB_SUB * BLOCK
    assert s % tq == 0
    return pl.pallas_call(
        _sb_kernel,
        grid=(b, pairs, s // tq),
        in_specs=[
            pl.BlockSpec((1, tq, LANES), lambda bi, p, qi: (bi, qi, q_col + p)),
            pl.BlockSpec((1, s, LANES), lambda bi, p, qi: (bi, 0, k_col + p)),
            pl.BlockSpec((1, s, LANES), lambda bi, p, qi: (bi, 0, v_col + p)),
            pl.BlockSpec((2 * BLOCK, BLOCK), lambda bi, p, qi: (0, 0)),
        ],
        out_specs=pl.BlockSpec((1, tq, LANES), lambda bi, p, qi: (bi, qi, p)),
        out_shape=jax.ShapeDtypeStruct((b, s, SB_HEADS * HEAD_DIM), _BF16),
        scratch_shapes=[
            pltpu.VMEM((SB_SUB, 2 * BLOCK, BLOCK), _F32),
            pltpu.VMEM((SB_SUB, 2 * BLOCK, LANES), _F32),
        ],
        compiler_params=pltpu.CompilerParams(
            dimension_semantics=("arbitrary", "arbitrary", "arbitrary"),
            vmem_limit_bytes=VMEM_LIMIT),
        name="sb_attn",
    )(qkv, qkv, qkv, _suffix_sum_matrix())


def _sw_bias_kernel(bucket_ref, rel_ref, sink_ref, o_ref):
    bucket = bucket_ref[...]
    col = lax.broadcasted_iota(jnp.int32, bucket.shape, 1)
    dist = lax.broadcasted_iota(jnp.int32, bucket.shape, 0) + BLOCK - col
    in_window = (dist >= 0) & (dist < BLOCK)
    for h in range(SW_HEADS):
        acc = jnp.zeros(bucket.shape, _F32)
        for b in range(NUM_BUCKETS):
            acc = jnp.where(bucket == b, rel_ref[b, h], acc)
        o_ref[h // SW_GROUP, (h % SW_GROUP) * BLOCK:(h % SW_GROUP + 1) * BLOCK, :] = jnp.where(
            col == 0, sink_ref[h], jnp.where(in_window, acc, MASKED))


def _t5_bucket(dist):
    max_exact = NUM_BUCKETS // 2
    d = np.maximum(dist, 1).astype(np.float32)
    large = max_exact + (np.log(d / max_exact) / math.log(MAX_DISTANCE / max_exact)
                         * (NUM_BUCKETS - max_exact)).astype(np.int32)
    large = np.minimum(large, NUM_BUCKETS - 1)
    return np.where(dist < max_exact, dist, large).astype(np.int32)


def _sw_bias(rel_bias, sinks):
    dist = (np.arange(BLOCK)[:, None] + BLOCK) - np.arange(2 * BLOCK)[None, :]
    bucket = jnp.asarray(_t5_bucket(np.maximum(dist, 0)))
    return pl.pallas_call(
        _sw_bias_kernel,
        in_specs=[
            pl.BlockSpec(memory_space=pltpu.VMEM),
            pl.BlockSpec(memory_space=pltpu.SMEM),
            pl.BlockSpec(memory_space=pltpu.SMEM),
        ],
        out_specs=pl.BlockSpec(memory_space=pltpu.VMEM),
        out_shape=jax.ShapeDtypeStruct((SW_KV_HEADS, SW_GROUP * BLOCK, 2 * BLOCK), _F32),
        name="sw_bias",
    )(bucket, rel_bias.astype(_F32), sinks.astype(_F32))


SW_SUB = 32
SW_GROUP = SW_HEADS // SW_KV_HEADS


def _sw_kernel(q_ref, kp_ref, kc_ref, vp_ref, vc_ref, bias_ref, o_ref):
    rows = SW_GROUP * BLOCK
    upper_lanes = lax.broadcasted_iota(jnp.int32, (rows, LANES), 1) >= HEAD_DIM
    kv_upper = lax.broadcasted_iota(jnp.int32, (2 * BLOCK, LANES), 1) >= HEAD_DIM
    not_sink = lax.broadcasted_iota(jnp.int32, (2 * BLOCK, LANES), 0) > 0
    kpos = lax.broadcasted_iota(jnp.int32, (rows, 2 * BLOCK), 1)
    first_ok = (kpos >= jnp.where(pl.program_id(1) > 0, 0, BLOCK)) | (kpos == 0)

    def kv_rows(prev_ref, cur_ref, blk, keep):
        cur = cur_ref[0, blk * BLOCK:(blk + 1) * BLOCK, :]
        prev = prev_ref[0] if blk == 0 else cur_ref[0, (blk - 1) * BLOCK:blk * BLOCK, :]
        kv = jnp.concatenate([prev, cur], axis=0)
        return jnp.where(keep, kv, jnp.zeros_like(kv))

    def logits(blk, g):
        q4 = jnp.concatenate(
            [q_ref[0, blk * BLOCK:(blk + 1) * BLOCK, t * LANES:(t + 1) * LANES]
             for t in range(SW_GROUP)], axis=0)
        keys = kv_rows(kp_ref, kc_ref, blk, not_sink & (kv_upper if g == 1 else ~kv_upper))
        return _dot_nt(q4, keys)

    def exponentials(z, blk, g):
        z = z + bias_ref[g]
        if blk == 0:
            z = jnp.where(first_ok, z, MASKED)
        return jnp.exp(z - jnp.max(z, axis=-1, keepdims=True)).astype(_BF16)

    def values(p, blk):
        vv = kv_rows(vp_ref, vc_ref, blk, not_sink)
        pv = _dot(p, jnp.concatenate([vv, jnp.ones_like(vv)], axis=1))
        return pv[:, :LANES] * (1.0 / pv[:, LANES:])

    items = [(blk, g) for blk in range(SW_SUB) for g in range(SW_KV_HEADS)]
    zs, ps, outs = {}, {}, {}
    for step in range(len(items) + 2):
        if step < len(items):
            zs[step] = logits(*items[step])
        if 0 <= step - 1 < len(items):
            ps[step - 1] = exponentials(zs.pop(step - 1), *items[step - 1])
        if 0 <= step - 2 < len(items):
            blk, g = items[step - 2]
            outs[g] = values(ps.pop(step - 2), blk)
            if g == SW_KV_HEADS - 1:
                y = jnp.where(upper_lanes, outs.pop(1), outs.pop(0)).astype(o_ref.dtype)
                for t in range(SW_GROUP):
                    o_ref[0, blk * BLOCK:(blk + 1) * BLOCK, t * LANES:(t + 1) * LANES] = (
                        y[t * BLOCK:(t + 1) * BLOCK])


def _sw_attn(qkv, table, *, q_col, k_col, v_col):
    b, s, _ = qkv.shape
    qw = SW_HEADS * HEAD_DIM
    tq = SW_SUB * BLOCK
    assert s % tq == 0 and SW_KV_HEADS * HEAD_DIM == LANES
    prev = lambda bi, n: (bi, jnp.maximum(n * SW_SUB - 1, 0))
    return pl.pallas_call(
        _sw_kernel,
        grid=(b, s // tq),
        in_specs=[
            pl.BlockSpec((1, tq, qw), lambda bi, n: (bi, n, q_col)),
            pl.BlockSpec((1, BLOCK, LANES), lambda bi, n: prev(bi, n) + (k_col,)),
            pl.BlockSpec((1, tq, LANES), lambda bi, n: (bi, n, k_col)),
            pl.BlockSpec((1, BLOCK, LANES), lambda bi, n: prev(bi, n) + (v_col,)),
            pl.BlockSpec((1, tq, LANES), lambda bi, n: (bi, n, v_col)),
            pl.BlockSpec((SW_KV_HEADS, SW_GROUP * BLOCK, 2 * BLOCK), lambda bi, n: (0, 0, 0)),
        ],
        out_specs=pl.BlockSpec((1, tq, qw), lambda bi, n: (bi, n, 0)),
        out_shape=jax.ShapeDtypeStruct((b, s, qw), _BF16),
        compiler_params=pltpu.CompilerParams(
            dimension_semantics=("arbitrary", "arbitrary"), vmem_limit_bytes=VMEM_LIMIT),
        name="sw_attn",
    )(qkv, qkv, qkv, qkv, qkv, table)


MIX_CHUNK = V7X_MXU_DIM


def _mix_kernel(x_ref, h_ref, wg_ref, ysb_ref, ysw_ref, wsb_ref, wsw_ref, wo_ref, o_ref,
                merged_scr):
    d = x_ref.shape[1]
    tm = x_ref.shape[0] // ROW_SUB
    n_chunks = d // MIX_CHUNK
    rows = lambda s: slice(s * tm, (s + 1) * tm)

    def projections(s, c):
        cols = slice(c * MIX_CHUNK, (c + 1) * MIX_CHUNK)
        gate_cols = slice(d + c * MIX_CHUNK, d + (c + 1) * MIX_CHUNK)
        h = h_ref[rows(s), :]
        return (_dot(h, wg_ref[:, cols]), _dot(ysb_ref[rows(s), :], wsb_ref[:, cols]),
                _dot(h, wg_ref[:, gate_cols]), _dot(ysw_ref[rows(s), :], wsw_ref[:, cols]))

    def epilogue(s, y):
        o_ref[rows(s), :] = x_ref[rows(s), :] + y

    ahead = projections(0, 0)
    pending = None
    for s in range(ROW_SUB):
        for c in range(n_chunks):
            logit_sb, y_sb, logit_sw, y_sw = ahead
            if c + 1 < n_chunks:
                ahead = projections(s, c + 1)
            elif s + 1 < ROW_SUB:
                ahead = projections(s + 1, 0)
            merged = jax.nn.sigmoid(logit_sb) * y_sb + jax.nn.sigmoid(logit_sw) * y_sw
            merged_scr[rows(s), c * MIX_CHUNK:(c + 1) * MIX_CHUNK] = merged.astype(_BF16)
            if c == 1 and pending is not None:
                epilogue(*pending)
                pending = None
        pending = (s, _dot(merged_scr[rows(s), :], wo_ref[...]))
    epilogue(*pending)


def _mix_out(x2d, h, w_gate, y_sb, y_sw, w_sb, w_sw, w_out, *, tm):
    n, d = x2d.shape
    assert d % MIX_CHUNK == 0
    step_rows = ROW_SUB * tm
    full = lambda a: pl.BlockSpec(a.shape, lambda i: (0, 0), pipeline_mode=pl.Buffered(1))
    rows = lambda a: pl.BlockSpec((step_rows, a.shape[1]), lambda i: (i, 0))
    return pl.pallas_call(
        _mix_kernel,
        grid=(n // step_rows,),
        in_specs=[rows(x2d), rows(h), full(w_gate), rows(y_sb), rows(y_sw),
                  full(w_sb), full(w_sw), full(w_out)],
        out_specs=pl.BlockSpec((step_rows, d), lambda i: (i, 0)),
        out_shape=jax.ShapeDtypeStruct((n, d), _F32),
        scratch_shapes=[pltpu.VMEM((step_rows, d), _BF16)],
        compiler_params=pltpu.CompilerParams(
            dimension_semantics=("arbitrary",), vmem_limit_bytes=VMEM_LIMIT),
        name="mix_out",
    )(x2d, h, w_gate, y_sb, y_sw, w_sb, w_sw, w_out)


FFN_SUB = 2
HALO = 8


def _ffn_kernel(x_ref, g_ref, wu_ref, cw_ref, cb_ref, wd_ref, gf_ref, o_ref,
                h_scr, act_scr, u_scr, carry_scr, *, steps_per_seq):
    tm = x_ref.shape[0] // FFN_SUB
    d_ff = wd_ref.shape[0]
    nj = d_ff // FF_CHUNK
    rows = lambda s: slice(s * tm, (s + 1) * tm)

    @pl.when(pl.program_id(0) % steps_per_seq == 0)
    def _():
        carry_scr[...] = jnp.zeros_like(carry_scr)

    def prologue(s):
        h_scr[rows(s), :] = _rms(x_ref[rows(s), :], g_ref[...]).astype(_BF16)

    def up_proj(s, j):
        h = h_scr[rows(s), :]
        gate = _dot(h, wu_ref[:, j * FF_CHUNK:(j + 1) * FF_CHUNK])
        value = _dot(h, wu_ref[:, d_ff + j * FF_CHUNK:d_ff + (j + 1) * FF_CHUNK])
        return gate, value

    def conv(slot, j, part, u):
        slabs = []
        for k in range(FF_CHUNK // LANES):
            slab = (slot * 2 + part) * (FF_CHUNK // LANES) + k
            cols = slice(part * d_ff + j * FF_CHUNK + k * LANES,
                         part * d_ff + j * FF_CHUNK + (k + 1) * LANES)
            lanes = slice((part * FF_CHUNK // LANES + k) * LANES,
                          (part * FF_CHUNK // LANES + k + 1) * LANES)
            uk = u[:, k * LANES:(k + 1) * LANES]
            u_scr[slab, HALO:HALO + tm, :] = uk
            u_scr[slab, 0:HALO, :] = carry_scr[j, :, lanes]
            carry_scr[j, :, lanes] = uk[tm - HALO:tm, :]
            y = cb_ref[:, cols] + cw_ref[CONV_WIDTH - 1:CONV_WIDTH, cols] * uk
            for tap in range(CONV_WIDTH - 1):
                back = CONV_WIDTH - 1 - tap
                y = y + cw_ref[tap:tap + 1, cols] * u_scr[slab, pl.ds(HALO - back, tm, stride=1), :]
            slabs.append(y)
        return jnp.concatenate(slabs, axis=1)

    def gated_conv(s, j, gate, value):
        slot = (s * nj + j) % 2
        act = jax.nn.silu(conv(slot, j, 0, gate)) * conv(slot, j, 1, value)
        act_scr[rows(s), j * FF_CHUNK:(j + 1) * FF_CHUNK] = act.astype(_BF16)

    def epilogue(s, y):
        o_ref[rows(s), :] = _rms(x_ref[rows(s), :] + y, gf_ref[...])

    prologue(0)
    ahead = up_proj(0, 0)
    pending = None
    for s in range(FFN_SUB):
        for j in range(nj):
            gate, value = ahead
            if j + 1 < nj:
                ahead = up_proj(s, j + 1)
            elif s + 1 < FFN_SUB:
                ahead = up_proj(s + 1, 0)
            gated_conv(s, j, gate, value)
            if j == 0 and s + 1 < FFN_SUB:
                prologue(s + 1)
            if j == 1 and pending is not None:
                epilogue(*pending)
                pending = None
        pending = (s, _dot(act_scr[rows(s), :], wd_ref[...]))
    epilogue(*pending)


def _conv_ffn(x1, g_ffn, w_up, conv_w, conv_b, w_down, g_final, *, tm, tiles_per_seq):
    n, d = x1.shape
    d_ff = w_down.shape[0]
    nj = d_ff // FF_CHUNK
    assert tiles_per_seq % FFN_SUB == 0
    step_rows = FFN_SUB * tm
    resident = lambda a: pl.BlockSpec(a.shape, lambda i: (0, 0), pipeline_mode=pl.Buffered(1))
    return pl.pallas_call(
        functools.partial(_ffn_kernel, steps_per_seq=tiles_per_seq // FFN_SUB),
        grid=(n // step_rows,),
        in_specs=[
            pl.BlockSpec((step_rows, d), lambda i: (i, 0)),
            resident(g_ffn), resident(w_up), resident(conv_w), resident(conv_b),
            resident(w_down), resident(g_final),
        ],
        out_specs=pl.BlockSpec((step_rows, d), lambda i: (i, 0)),
        out_shape=jax.ShapeDtypeStruct((n, d), _F32),
        scratch_shapes=[
            pltpu.VMEM((step_rows, d), _BF16),
            pltpu.VMEM((step_rows, d_ff), _BF16),
            pltpu.VMEM((2 * 2 * FF_CHUNK // LANES, HALO + tm, LANES), _F32),
            pltpu.VMEM((nj, HALO, 2 * FF_CHUNK), _F32),
        ],
        compiler_params=pltpu.CompilerParams(
            dimension_semantics=("arbitrary",), vmem_limit_bytes=VMEM_LIMIT),
        name="conv_ffn",
    )(x1, g_ffn, w_up, conv_w, conv_b, w_down, g_final)


def kernel(x, g_mix, w_in, w_sb_proj, w_sw_proj, w_out, rel_bias, sinks, g_ffn, w_up,
           conv_w, conv_b, w_down, g_final):
    b, s, d = x.shape
    n = b * s
    d_ff = w_down.shape[0]
    sb_w = SB_HEADS * HEAD_DIM
    sw_qw = SW_HEADS * HEAD_DIM
    sw_kw = SW_KV_HEADS * HEAD_DIM
    assert s % BLOCK == 0 and d_ff % FF_CHUNK == 0
    tm = min(ROW_TILE, s)
    assert s % (max(ROW_SUB, FFN_SUB) * tm) == 0

    qkv_end = 3 * sb_w + sw_qw
    order = np.arange(SW_HEADS).reshape(SW_KV_HEADS, SW_GROUP).T.reshape(-1)
    by_head = lambda a, axis: jnp.take(
        a.reshape(a.shape[:axis] + (SW_HEADS, HEAD_DIM) + a.shape[axis + 1:]), order,
        axis=axis).reshape(a.shape)
    w_in = w_in.astype(_BF16)
    w_qkv = jnp.concatenate(
        [w_in[:, :3 * sb_w], by_head(w_in[:, 3 * sb_w:qkv_end], 1),
         w_in[:, qkv_end:qkv_end + 2 * sw_kw]], axis=1)
    w_gate = w_in[:, qkv_end + 2 * sw_kw:]
    row = lambda a: a.reshape(1, -1).astype(_F32)

    x2d = x.reshape(n, d)
    assert sb_w == IN_CHUNK and sw_qw == IN_CHUNK
    qkv, h = _in_proj(x2d, row(g_mix), w_qkv, tm=tm, chunk=IN_CHUNK,
                      n_scaled=(0, 3 * sb_w // IN_CHUNK))
    qkv = qkv.reshape(b, s, -1)
    y_sb = _sb_attn(qkv, q_col=0, k_col=sb_w // LANES, v_col=2 * sb_w // LANES)
    y_sw = _sw_attn(qkv, _sw_bias(rel_bias, sinks), q_col=3 * sb_w // sw_qw,
                    k_col=qkv_end // LANES, v_col=qkv_end // LANES + 1)
    x1 = _mix_out(x2d, h, w_gate, y_sb.reshape(n, sb_w), y_sw.reshape(n, sw_qw),
                  w_sb_proj.astype(_BF16), by_head(w_sw_proj, 0).astype(_BF16),
                  w_out.astype(_BF16), tm=tm)
    out = _conv_ffn(x1, row(g_ffn), w_up.astype(_BF16), conv_w.astype(_F32), row(conv_b),
                    w_down.astype(_BF16), row(g_final), tm=tm, tiles_per_seq=s // tm)
    return out.reshape(b, s, d)
```
